```python
import jax, jax.numpy as jnp
from jax import lax
import numpy as np

D_MODEL = 1024
BATCH = 8
SEQ = 4096
DEPTH = 1

D_A = D_MODEL
H_A = 8
GROUP_A = D_A // H_A
CHUNK_A = 128
H_B = 4
KEY_B = D_MODEL // 2
VAL_B = D_MODEL
DK_B = KEY_B // H_B
DV_B = VAL_B // H_B
GATE_RANK = 16
GATE_NORM = 16.0
CHUNK_B = 64
EPS = 1e-6
LN_EPS = 1e-5

N_IN = 3 * D_A + 2 * KEY_B + 2 * VAL_B + GATE_RANK + 2 * D_MODEL
SPLIT_POINTS = (
    D_A,
    2 * D_A,
    3 * D_A,
    3 * D_A + KEY_B,
    3 * D_A + 2 * KEY_B,
    3 * D_A + 2 * KEY_B + VAL_B,
    3 * D_A + 2 * KEY_B + 2 * VAL_B,
    3 * D_A + 2 * KEY_B + 2 * VAL_B + GATE_RANK,
)

kernel_name = 'hybrid_gmlp_gla_gated_parallel'


def _rmsnorm(x, g):
    xf = x.astype(jnp.float32)
    y = xf * lax.rsqrt(jnp.mean(xf * xf, axis=-1, keepdims=True) + EPS)
    return (y * g.astype(jnp.float32)).astype(x.dtype)


def _layernorm(x, g, b):
    xf = x.astype(jnp.float32)
    mu = jnp.mean(xf, axis=-1, keepdims=True)
    xc = xf - mu
    y = xc * lax.rsqrt(jnp.mean(xc * xc, axis=-1, keepdims=True) + LN_EPS)
    return (y * g.astype(jnp.float32) + b.astype(jnp.float32)).astype(x.dtype)


def _spatial_gating(u, v, ln_g, ln_b, w_s, b_s):
    bsz, seq, _ = u.shape
    n_chunks = seq // CHUNK_A
    v = _layernorm(v, ln_g, ln_b)
    vc = v.reshape(bsz, n_chunks, CHUNK_A, H_A, GROUP_A)
    causal = jnp.tril(jnp.ones((CHUNK_A, CHUNK_A), dtype=bool))
    w = jnp.where(causal, w_s, jnp.zeros_like(w_s)).astype(v.dtype)
    mixed = jnp.einsum('hts,bnshc->bnthc', w, vc) + b_s.T.astype(v.dtype)[None, None, :, :, None]
    return u * mixed.reshape(bsz, seq, D_A)


def _gla_chunk_step(state, inp):
    q, k, v, g = inp
    b = jnp.cumsum(g, axis=1)
    b_last = b[:, -1]
    b_mid = b[:, CHUNK_B // 2 - 1][:, None]
    q_i = q * jnp.exp(b - b_mid)
    k_i = k * jnp.exp(b_mid - b)
    scores = jnp.einsum('bthk,bshk->bhts', q_i, k_i)
    causal = jnp.tril(jnp.ones((CHUNK_B, CHUNK_B), dtype=bool))
    scores = jnp.where(causal, scores, 0.0)
    o = (jnp.einsum('bhts,bshv->bthv', scores, v)
         + jnp.einsum('bthk,bhkv->bthv', q * jnp.exp(b), state))
    k_s = k * jnp.exp(b_last[:, None] - b)
    state = jnp.exp(b_last)[..., None] * state + jnp.einsum('bshk,bshv->bhkv', k_s, v)
    return state, o


def _gla(q, k, v, g):
    bsz, seq, _ = q.shape
    n_chunks = seq // CHUNK_B

    def to_chunks(t, d):
        t = t.astype(jnp.float32).reshape(bsz, n_chunks, CHUNK_B, H_B, d)
        return jnp.transpose(t, (1, 0, 2, 3, 4))

    qc = to_chunks(q, DK_B) * (DK_B ** -0.5)
    kc = to_chunks(k, DK_B)
    vc = to_chunks(v, DV_B)
    gc = to_chunks(g, DK_B)
    state0 = jnp.zeros((bsz, H_B, DK_B, DV_B), jnp.float32)
    _, o = lax.scan(_gla_chunk_step, state0, (qc, kc, vc, gc))
    o = jnp.transpose(o, (1, 0, 2, 3, 4)).reshape(bsz, seq, H_B, DV_B)
    return o


def setup_inputs(seed: int = 0) -> dict:
    key = jax.random.key(seed)
    ks = jax.random.split(key, 16)
    f32 = jnp.float32
    nrm = lambda k, shape, s: jax.random.normal(k, shape, f32) * s
    return {
        'x': nrm(ks[0], (BATCH, SEQ, D_MODEL), 1.0),
        'norm_g': 1.0 + nrm(ks[1], (DEPTH, D_MODEL), 0.02),
        'w_in': nrm(ks[2], (DEPTH, D_MODEL, N_IN), D_MODEL ** -0.5),
        'ln_v_g': 1.0 + nrm(ks[3], (DEPTH, D_A), 0.02),
        'ln_v_b': nrm(ks[4], (DEPTH, D_A), 0.02),
        'w_spatial': nrm(ks[5], (DEPTH, H_A, CHUNK_A, CHUNK_A), 0.5 * CHUNK_A ** -0.5),
        'b_spatial': 1.0 + nrm(ks[6], (DEPTH, H_A, CHUNK_A), 0.02),
        'w_gate_up': nrm(ks[7], (DEPTH, GATE_RANK, KEY_B), GATE_RANK ** -0.5),
        'b_gate_up': nrm(ks[8], (DEPTH, KEY_B), 0.01),
        'gla_norm_g': 1.0 + nrm(ks[9], (DEPTH, DV_B), 0.02),
        'w_branch_a': nrm(ks[10], (DEPTH, D_A, D_MODEL), D_A ** -0.5),
        'w_branch_b': nrm(ks[11], (DEPTH, VAL_B, D_MODEL), VAL_B ** -0.5),
        'w_out': nrm(ks[12], (DEPTH, D_MODEL, D_MODEL), D_MODEL ** -0.5),
        'final_norm_g': 1.0 + nrm(ks[13], (D_MODEL,), 0.02),
    }


def reference(x, norm_g, w_in, ln_v_g, ln_v_b, w_spatial, b_spatial, w_gate_up, b_gate_up,
              gla_norm_g, w_branch_a, w_branch_b, w_out, final_norm_g):
    bsz, seq, _ = x.shape
    for l in range(DEPTH):
        h = _rmsnorm(x, norm_g[l])
        proj = h @ w_in[l]
        u, v, z_a, q, k, v_b, z_b, lr, gates = jnp.split(proj, SPLIT_POINTS, axis=-1)
        a = _spatial_gating(jax.nn.gelu(u, approximate=False), jax.nn.gelu(v, approximate=False),
                            ln_v_g[l], ln_v_b[l], w_spatial[l], b_spatial[l])
        a = a * jax.nn.silu(z_a)
        logit = (lr @ w_gate_up[l] + b_gate_up[l]).astype(jnp.float32)
        log_alpha = jax.nn.log_sigmoid(logit) / GATE_NORM
        o = _gla(q, k, v_b, log_alpha)
        o = _rmsnorm(o, gla_norm_g[l]).astype(x.dtype).reshape(bsz, seq, VAL_B)
        o = o * jax.nn.silu(z_b)
        g_a, g_b = jnp.split(jax.nn.sigmoid(gates), 2, axis=-1)
        merged = g_a * (a @ w_branch_a[l]) + g_b * (o @ w_branch_b[l])
        x = x + merged @ w_out[l]
    return _rmsnorm(x, final_norm_g)
```

```python
import functools
import math

import jax
import jax.numpy as jnp
from jax import lax
from jax.experimental import pallas as pl
from jax.experimental.pallas import tpu as pltpu

D_MODEL = 1024
D_A = D_MODEL
H_A = 8
GROUP_A = D_A // H_A
CHUNK_A = 128
H_B = 4
KEY_B = D_MODEL // 2
VAL_B = D_MODEL
DK_B = KEY_B // H_B
DV_B = VAL_B // H_B
GATE_RANK = 16
GATE_NORM = 16.0
CHUNK_B = 64
EPS = 1e-6
LN_EPS = 1e-5

V7X_LANES = 128
V7X_VMEM_BYTES = 64 * 1024 * 1024
TILE_S = 256
LR_PAD = V7X_LANES

N_A = 3 * D_A
N_B = 2 * KEY_B + 2 * VAL_B
N_G = 2 * D_MODEL


def _bf(x):
    return x.astype(jnp.bfloat16)


def _dot(a, b):
    return jnp.dot(a, b, preferred_element_type=jnp.float32)


def _dot_nt(a, b):
    return lax.dot_general(a, b, (((1,), (1,)), ((), ())), preferred_element_type=jnp.float32)


def _gelu(x):
    return 0.5 * x * (1.0 + lax.erf(x * (1.0 / math.sqrt(2.0))))


def _sigmoid(x):
    return 1.0 / (1.0 + jnp.exp(-x))


def _silu(x):
    return x * _sigmoid(x)


def _log_sigmoid(x):
    return jnp.minimum(x, 0.0) - jnp.log1p(jnp.exp(-jnp.abs(x)))


def _rms(x, g):
    return x * lax.rsqrt(jnp.mean(x * x, axis=-1, keepdims=True) + EPS) * g


def _layer_kernel(x_ref, norm_g_ref, w_a_ref, w_b_ref, w_g_ref, ln_g_ref, ln_b_ref, w_s_ref, b_s_ref,
                  w_gu_ref, b_gu_ref, gla_g_ref, w_ba_ref, w_bb_ref, w_out_ref, fin_g_ref,
                  out_ref, state_ref, a_ref, o_ref):
    @pl.when(pl.program_id(1) == 0)
    def _():
        state_ref[...] = jnp.zeros_like(state_ref)

    x = x_ref[...]
    hb = _bf(_rms(x, norm_g_ref[...]))

    pa = _dot(hb, w_a_ref[...])
    u = _gelu(pa[:, :D_A])
    v = _gelu(pa[:, D_A:2 * D_A])
    z_a = pa[:, 2 * D_A:]
    mu = jnp.mean(v, axis=-1, keepdims=True)
    vc = v - mu
    vn = vc * lax.rsqrt(jnp.mean(vc * vc, axis=-1, keepdims=True) + LN_EPS) * ln_g_ref[...] + ln_b_ref[...]
    vnb = _bf(vn)
    gate_a = u * _silu(z_a)
    row = lax.broadcasted_iota(jnp.int32, (CHUNK_A, CHUNK_A), 0)
    col = lax.broadcasted_iota(jnp.int32, (CHUNK_A, CHUNK_A), 1)
    causal_a = row >= col
    for g in range(H_A):
        w_g = _bf(jnp.where(causal_a, w_s_ref[g], 0.0))
        cs = slice(g * GROUP_A, (g + 1) * GROUP_A)
        for c in range(TILE_S // CHUNK_A):
            rs = slice(c * CHUNK_A, (c + 1) * CHUNK_A)
            mixed = _dot(w_g, vnb[rs, cs]) + b_s_ref[:, cs]
            a_ref[rs, cs] = _bf(gate_a[rs, cs] * mixed)

    pb = _dot(hb, w_b_ref[...])
    q = pb[:, :KEY_B] * (DK_B ** -0.5)
    k = pb[:, KEY_B:2 * KEY_B]
    v_b = pb[:, 2 * KEY_B:2 * KEY_B + VAL_B]
    z_b = pb[:, 2 * KEY_B + VAL_B:N_B]
    lr = pb[:, N_B:]
    logit = _dot(_bf(lr), w_gu_ref[...]) + b_gu_ref[...]
    log_alpha = _log_sigmoid(logit) * (1.0 / GATE_NORM)

    ri = lax.broadcasted_iota(jnp.int32, (CHUNK_B, CHUNK_B), 0)
    ci = lax.broadcasted_iota(jnp.int32, (CHUNK_B, CHUNK_B), 1)
    causal_b = ri >= ci
    tril = _bf(jnp.where(causal_b, 1.0, 0.0))
    gla_g = gla_g_ref[...]
    for c in range(TILE_S // CHUNK_B):
        rs = slice(c * CHUNK_B, (c + 1) * CHUNK_B)
        g_c = log_alpha[rs]
        g_hi = _bf(g_c)
        g_lo = _bf(g_c - g_hi.astype(jnp.float32))
        b = _dot(tril, g_hi) + _dot(tril, g_lo)
        b_mid = b[CHUNK_B // 2 - 1:CHUNK_B // 2]
        b_last = b[CHUNK_B - 1:CHUNK_B]
        q_c = q[rs]
        k_c = k[rs]
        q_i = _bf(q_c * jnp.exp(b - b_mid))
        k_i = _bf(k_c * jnp.exp(b_mid - b))
        q_s = _bf(q_c * jnp.exp(b))
        k_s = _bf(k_c * jnp.exp(b_last - b))
        decay = jnp.exp(b_last)
        for h in range(H_B):
            ks = slice(h * DK_B, (h + 1) * DK_B)
            vs = slice(h * DV_B, (h + 1) * DV_B)
            v_h = v_b[rs, vs]
            v_hb = _bf(v_h)
            scores = jnp.where(causal_b, _dot_nt(q_i[:, ks], k_i[:, ks]), 0.0)
            st = state_ref[h]
            o_h = _dot(_bf(scores), v_hb) + _dot_nt(q_s[:, ks], _bf(st))
            state_ref[h] = st * decay[:, ks] + _dot(_bf(v_h.T), k_s[:, ks])
            o_n = _rms(o_h, gla_g)
            o_ref[rs, vs] = _bf(o_n * _silu(z_b[rs, vs]))

    pg = _dot(hb, w_g_ref[...])
    g_a = _sigmoid(pg[:, :D_MODEL])
    g_b = _sigmoid(pg[:, D_MODEL:])
    merged = g_a * _dot(a_ref[...], w_ba_ref[...]) + g_b * _dot(o_ref[...], w_bb_ref[...])
    y = x + _dot(_bf(merged), w_out_ref[...])
    out_ref[...] = _rms(y, fin_g_ref[...])


def _const_spec(shape):
    nd = len(shape)
    return pl.BlockSpec(shape, lambda b, s: (0,) * nd, pipeline_mode=pl.Buffered(1))


def _vmem_limit_bytes():
    weights = 2 * (D_MODEL * (N_A + N_B + LR_PAD + N_G) + 3 * D_MODEL * D_MODEL + LR_PAD * KEY_B)
    weights += 4 * (H_A * CHUNK_A * CHUNK_A + CHUNK_A * D_A)
    io_windows = 2 * 2 * TILE_S * D_MODEL * 4
    scratch = H_B * DV_B * DK_B * 4 + 2 * TILE_S * D_MODEL * 2
    temporaries = 6 * TILE_S * (N_A + N_B + LR_PAD + N_G) * 4
    return min(weights + io_windows + scratch + temporaries, V7X_VMEM_BYTES - 8 * 1024 * 1024)


@jax.jit
def _forward(x, norm_g, w_in, ln_v_g, ln_v_b, w_spatial, b_spatial, w_gate_up, b_gate_up,
             gla_norm_g, w_branch_a, w_branch_b, w_out, final_norm_g):
    bsz, seq, d = x.shape
    assert d == D_MODEL and seq % TILE_S == 0
    assert TILE_S % CHUNK_A == 0 and TILE_S % CHUNK_B == 0
    bf = jnp.bfloat16
    w_a = w_in[:, :N_A].astype(bf)
    w_b = jnp.pad(w_in[:, N_A:N_A + N_B + GATE_RANK], ((0, 0), (0, LR_PAD - GATE_RANK))).astype(bf)
    w_g = w_in[:, N_A + N_B + GATE_RANK:].astype(bf)
    w_gu = jnp.pad(w_gate_up, ((0, LR_PAD - GATE_RANK), (0, 0))).astype(bf)
    b_s = jnp.repeat(b_spatial.T, GROUP_A, axis=1)
    row = lambda a: a.reshape(1, -1)

    operands = (
        x, row(norm_g), w_a, w_b, w_g, row(ln_v_g), row(ln_v_b), w_spatial, b_s,
        w_gu, row(b_gate_up), row(gla_norm_g), w_branch_a.astype(bf), w_branch_b.astype(bf),
        w_out.astype(bf), row(final_norm_g),
    )
    in_specs = [pl.BlockSpec((None, TILE_S, D_MODEL), lambda b, s: (b, s, 0))]
    in_specs += [_const_spec(op.shape) for op in operands[1:]]
    return pl.pallas_call(
        _layer_kernel,
        grid=(bsz, seq // TILE_S),
        in_specs=in_specs,
        out_specs=pl.BlockSpec((None, TILE_S, D_MODEL), lambda b, s: (b, s, 0)),
        out_shape=jax.ShapeDtypeStruct(x.shape, x.dtype),
        scratch_shapes=[
            pltpu.VMEM((H_B, DV_B, DK_B), jnp.float32),
            pltpu.VMEM((TILE_S, D_MODEL), jnp.bfloat16),
            pltpu.VMEM((TILE_S, D_MODEL), jnp.bfloat16),
        ],
        compiler_params=pltpu.CompilerParams(
            dimension_semantics=("arbitrary", "arbitrary"),
            vmem_limit_bytes=_vmem_limit_bytes(),
        ),
        name="hybrid_layer",
    )(*operands)


def kernel(x, norm_g, w_in, ln_v_g, ln_v_b, w_spatial, b_spatial, w_gate_up, b_gate_up, gla_norm_g,
           w_branch_a, w_branch_b, w_out, final_norm_g):
    assert norm_g.shape[0] == 1, "single-layer problem"
    return _forward(x, norm_g[0], w_in[0], ln_v_g[0], ln_v_b[0], w_spatial[0], b_spatial[0],
                    w_gate_up[0], b_gate_up[0], gla_norm_g[0], w_branch_a[0], w_branch_b[0],
                    w_out[0], final_norm_g)
```

```python
import functools
import math

import jax
import jax.numpy as jnp
from jax import lax
from jax.experimental import pallas as pl
from jax.experimental.pallas import tpu as pltpu

D_MODEL = 1024
D_A = D_MODEL
H_A = 8
GROUP_A = D_A // H_A
CHUNK_A = 128
H_B = 4
KEY_B = D_MODEL // 2
VAL_B = D_MODEL
DK_B = KEY_B // H_B
DV_B = VAL_B // H_B
GATE_RANK = 16
GATE_NORM = 16.0
CHUNK_B = 64
EPS = 1e-6
LN_EPS = 1e-5

V7X_LANES = 128
V7X_VMEM_BYTES = 64 * 1024 * 1024
TILE_S = 256
LR_PAD = V7X_LANES

N_A = 3 * D_A
N_B = 2 * KEY_B + 2 * VAL_B
N_G = 2 * D_MODEL


def _bf(x):
    return x.astype(jnp.bfloat16)


def _dot(a, b):
    return jnp.dot(a, b, preferred_element_type=jnp.float32)


def _dot_nt(a, b):
    return lax.dot_general(a, b, (((1,), (1,)), ((), ())), preferred_element_type=jnp.float32)


def _gelu(x):
    return 0.5 * x * (1.0 + lax.erf(x * (1.0 / math.sqrt(2.0))))


def _sigmoid(x):
    return 1.0 / (1.0 + jnp.exp(-x))


def _silu(x):
    return x * _sigmoid(x)


def _log_sigmoid(x):
    return jnp.minimum(x, 0.0) - jnp.log1p(jnp.exp(-jnp.abs(x)))


def _rms(x, g):
    return x * lax.rsqrt(jnp.mean(x * x, axis=-1, keepdims=True) + EPS) * g


def _layer_kernel(x_ref, norm_g_ref, w_a_ref, w_b_ref, w_g_ref, ln_g_ref, ln_b_ref, w_s_ref, b_s_ref,
                  w_gu_ref, b_gu_ref, gla_g_ref, w_ba_ref, w_bb_ref, w_out_ref, fin_g_ref,
                  out_ref, state_ref, a_ref, o_ref):
    @pl.when(pl.program_id(1) == 0)
    def _():
        state_ref[...] = jnp.zeros_like(state_ref)

    x = x_ref[...]
    hb = _bf(_rms(x, norm_g_ref[...]))

    pa = _dot(hb, w_a_ref[...])
    u = _gelu(pa[:, :D_A])
    v = _gelu(pa[:, D_A:2 * D_A])
    z_a = pa[:, 2 * D_A:]
    mu = jnp.mean(v, axis=-1, keepdims=True)
    vc = v - mu
    vn = vc * lax.rsqrt(jnp.mean(vc * vc, axis=-1, keepdims=True) + LN_EPS) * ln_g_ref[...] + ln_b_ref[...]
    vnb = _bf(vn)
    gate_a = u * _silu(z_a)
    row = lax.broadcasted_iota(jnp.int32, (CHUNK_A, CHUNK_A), 0)
    col = lax.broadcasted_iota(jnp.int32, (CHUNK_A, CHUNK_A), 1)
    causal_a = row >= col
    for g in range(H_A):
        w_g = _bf(jnp.where(causal_a, w_s_ref[g], 0.0))
        cs = slice(g * GROUP_A, (g + 1) * GROUP_A)
        for c in range(TILE_S // CHUNK_A):
            rs = slice(c * CHUNK_A, (c + 1) * CHUNK_A)
            mixed = _dot(w_g, vnb[rs, cs]) + b_s_ref[:, cs]
            a_ref[rs, cs] = _bf(gate_a[rs, cs] * mixed)

    pb = _dot(hb, w_b_ref[...])
    q = pb[:, :KEY_B] * (DK_B ** -0.5)
    k = pb[:, KEY_B:2 * KEY_B]
    v_b = pb[:, 2 * KEY_B:2 * KEY_B + VAL_B]
    z_b = pb[:, 2 * KEY_B + VAL_B:N_B]
    lr = pb[:, N_B:]
    logit = _dot(_bf(lr), w_gu_ref[...]) + b_gu_ref[...]
    log_alpha = _log_sigmoid(logit) * (1.0 / GATE_NORM)

    n_cb = TILE_S // CHUNK_B
    ri = lax.broadcasted_iota(jnp.int32, (TILE_S, TILE_S), 0)
    ci = lax.broadcasted_iota(jnp.int32, (TILE_S, TILE_S), 1)
    tril = _bf(jnp.where((ri >= ci) & (ri // CHUNK_B == ci // CHUNK_B), 1.0, 0.0))
    g_hi = _bf(log_alpha)
    g_lo = _bf(log_alpha - g_hi.astype(jnp.float32))
    b = _dot(tril, g_hi) + _dot(tril, g_lo)
    rows = lambda t, r: jnp.broadcast_to(t[r:r + 1], (CHUNK_B, KEY_B))
    b_mid = jnp.concatenate([rows(b, c * CHUNK_B + CHUNK_B // 2 - 1) for c in range(n_cb)], axis=0)
    b_last = jnp.concatenate([rows(b, c * CHUNK_B + CHUNK_B - 1) for c in range(n_cb)], axis=0)
    q_i = _bf(q * jnp.exp(b - b_mid))
    k_i = _bf(k * jnp.exp(b_mid - b))
    q_s = _bf(q * jnp.exp(b))
    k_s = _bf(k * jnp.exp(b_last - b))
    v_bb = _bf(v_b)
    rc = lax.broadcasted_iota(jnp.int32, (CHUNK_B, CHUNK_B), 0)
    cc = lax.broadcasted_iota(jnp.int32, (CHUNK_B, CHUNK_B), 1)
    causal_b = rc >= cc
    gla_g = gla_g_ref[...]
    o_intra = {}
    for c in range(n_cb):
        rs = slice(c * CHUNK_B, (c + 1) * CHUNK_B)
        for h in range(H_B):
            ks = slice(h * DK_B, (h + 1) * DK_B)
            vs = slice(h * DV_B, (h + 1) * DV_B)
            scores = jnp.where(causal_b, _dot_nt(q_i[rs, ks], k_i[rs, ks]), 0.0)
            o_intra[c, h] = _dot(_bf(scores), v_bb[rs, vs])
    for c in range(n_cb):
        rs = slice(c * CHUNK_B, (c + 1) * CHUNK_B)
        decay = jnp.exp(b[c * CHUNK_B + CHUNK_B - 1:(c + 1) * CHUNK_B])
        for h in range(H_B):
            ks = slice(h * DK_B, (h + 1) * DK_B)
            vs = slice(h * DV_B, (h + 1) * DV_B)
            st = state_ref[h]
            o_h = o_intra[c, h] + _dot_nt(q_s[rs, ks], _bf(st))
            state_ref[h] = st * decay[:, ks] + _dot(_bf(v_b[rs, vs].T), k_s[rs, ks])
            o_n = _rms(o_h, gla_g)
            o_ref[rs, vs] = _bf(o_n * _silu(z_b[rs, vs]))

    pg = _dot(hb, w_g_ref[...])
    g_a = _sigmoid(pg[:, :D_MODEL])
    g_b = _sigmoid(pg[:, D_MODEL:])
    merged = g_a * _dot(a_ref[...], w_ba_ref[...]) + g_b * _dot(o_ref[...], w_bb_ref[...])
    y = x + _dot(_bf(merged), w_out_ref[...])
    out_ref[...] = _rms(y, fin_g_ref[...])


def _const_spec(shape):
    nd = len(shape)
    return pl.BlockSpec(shape, lambda b, s: (0,) * nd, pipeline_mode=pl.Buffered(1))


def _vmem_limit_bytes():
    weights = 2 * (D_MODEL * (N_A + N_B + LR_PAD + N_G) + 3 * D_MODEL * D_MODEL + LR_PAD * KEY_B)
    weights += 4 * (H_A * CHUNK_A * CHUNK_A + CHUNK_A * D_A)
    io_windows = 2 * 2 * TILE_S * D_MODEL * 4
    scratch = H_B * DV_B * DK_B * 4 + 2 * TILE_S * D_MODEL * 2
    temporaries = 6 * TILE_S * (N_A + N_B + LR_PAD + N_G) * 4
    return min(weights + io_windows + scratch + temporaries, V7X_VMEM_BYTES - 8 * 1024 * 1024)


@jax.jit
def _forward(x, norm_g, w_in, ln_v_g, ln_v_b, w_spatial, b_spatial, w_gate_up, b_gate_up,
             gla_norm_g, w_branch_a, w_branch_b, w_out, final_norm_g):
    bsz, seq, d = x.shape
    assert d == D_MODEL and seq % TILE_S == 0
    assert TILE_S % CHUNK_A == 0 and TILE_S % CHUNK_B == 0
    bf = jnp.bfloat16
    w_a = w_in[:, :N_A].astype(bf)
    w_b = jnp.pad(w_in[:, N_A:N_A + N_B + GATE_RANK], ((0, 0), (0, LR_PAD - GATE_RANK))).astype(bf)
    w_g = w_in[:, N_A + N_B + GATE_RANK:].astype(bf)
    w_gu = jnp.pad(w_gate_up, ((0, LR_PAD - GATE_RANK), (0, 0))).astype(bf)
    b_s = jnp.repeat(b_spatial.T, GROUP_A, axis=1)
    row = lambda a: a.reshape(1, -1)

    operands = (
        x, row(norm_g), w_a, w_b, w_g, row(ln_v_g), row(ln_v_b), w_spatial, b_s,
        w_gu, row(b_gate_up), row(gla_norm_g), w_branch_a.astype(bf), w_branch_b.astype(bf),
        w_out.astype(bf), row(final_norm_g),
    )
    in_specs = [pl.BlockSpec((None, TILE_S, D_MODEL), lambda b, s: (b, s, 0))]
    in_specs += [_const_spec(op.shape) for op in operands[1:]]
    return pl.pallas_call(
        _layer_kernel,
        grid=(bsz, seq // TILE_S),
        in_specs=in_specs,
        out_specs=pl.BlockSpec((None, TILE_S, D_MODEL), lambda b, s: (b, s, 0)),
        out_shape=jax.ShapeDtypeStruct(x.shape, x.dtype),
        scratch_shapes=[
            pltpu.VMEM((H_B, DV_B, DK_B), jnp.float32),
            pltpu.VMEM((TILE_S, D_MODEL), jnp.bfloat16),
            pltpu.VMEM((TILE_S, D_MODEL), jnp.bfloat16),
        ],
        compiler_params=pltpu.CompilerParams(
            dimension_semantics=("arbitrary", "arbitrary"),
            vmem_limit_bytes=_vmem_limit_bytes(),
        ),
        name="hybrid_layer",
    )(*operands)


def kernel(x, norm_g, w_in, ln_v_g, ln_v_b, w_spatial, b_spatial, w_gate_up, b_gate_up, gla_norm_g,
           w_branch_a, w_branch_b, w_out, final_norm_g):
    assert norm_g.shape[0] == 1, "single-layer problem"
    return _forward(x, norm_g[0], w_in[0], ln_v_g[0], ln_v_b[0], w_spatial[0], b_spatial[0],
                    w_gate_up[0], b_gate_up[0], gla_norm_g[0], w_branch_a[0], w_branch_b[0],
                    w_out[0], final_norm_g)
```

```python
import math

import jax
import jax.numpy as jnp
from jax import lax
from jax.experimental import pallas as pl
from jax.experimental.pallas import tpu as pltpu

D_MODEL = 1024
D_A = D_MODEL
H_A = 8
GROUP_A = D_A // H_A
CHUNK_A = 128
H_B = 4
KEY_B = D_MODEL // 2
VAL_B = D_MODEL
DK_B = KEY_B // H_B
DV_B = VAL_B // H_B
GATE_RANK = 16
GATE_NORM = 16.0
CHUNK_B = 64
EPS = 1e-6
LN_EPS = 1e-5

V7X_LANES = 128
V7X_VMEM_BYTES = 64 * 1024 * 1024
TILE_S = 256
LR_PAD = V7X_LANES
PIECE_N = 512

N_A = 3 * D_A
N_B = 2 * KEY_B + 2 * VAL_B
N_BP = N_B + LR_PAD
N_G = 2 * D_MODEL
N_CA = TILE_S // CHUNK_A
N_CB = TILE_S // CHUNK_B


def _bf(x):
    return x.astype(jnp.bfloat16)


def _dot(a, b):
    return jnp.dot(a, b, preferred_element_type=jnp.float32)


def _dot_nt(a, b):
    return lax.dot_general(a, b, (((1,), (1,)), ((), ())), preferred_element_type=jnp.float32)


def _gelu(x):
    return 0.5 * x * (1.0 + lax.erf(x * (1.0 / math.sqrt(2.0))))


def _sigmoid(x):
    return 1.0 / (1.0 + jnp.exp(-x))


def _silu(x):
    return x * _sigmoid(x)


def _log_sigmoid(x):
    return jnp.minimum(x, 0.0) - jnp.log1p(jnp.exp(-jnp.abs(x)))


def _rms(x, g):
    return x * lax.rsqrt(jnp.mean(x * x, axis=-1, keepdims=True) + EPS) * g


def _layer_kernel(x_ref, norm_g_ref, w_a_ref, w_b_ref, w_g_ref, ln_g_ref, ln_b_ref, w_s_ref, b_s_ref,
                  w_gu_ref, b_gu_ref, gla_g_ref, w_ba_ref, w_bb_ref, w_out_ref, fin_g_ref,
                  out_ref,
                  state_ref, hb_ref, pa_ref, pb_ref, gate_ref, vn_ref, ga_ref, gh_ref, gl_ref, b_ref,
                  qi_ref, ki_ref, qs_ref, ks_ref, a_ref, o_ref, ba_ref, mg_ref):
    @pl.when(pl.program_id(1) == 0)
    def _():
        state_ref[...] = jnp.zeros_like(state_ref)

    def norm_in():
        hb_ref[...] = _bf(_rms(x_ref[...], norm_g_ref[...]))

    def proj(dst_ref, w_ref, lo, hi):
        def task():
            dst_ref[:, lo:hi] = _dot(hb_ref[...], w_ref[:, lo:hi])
        return task

    def act_v(r):
        def task():
            rs = slice(r * CHUNK_B, (r + 1) * CHUNK_B)
            v = _gelu(pa_ref[rs, D_A:2 * D_A])
            mu = jnp.mean(v, axis=-1, keepdims=True)
            vc = v - mu
            inv = lax.rsqrt(jnp.mean(vc * vc, axis=-1, keepdims=True) + LN_EPS)
            vn_ref[rs, :] = _bf(vc * inv * ln_g_ref[...] + ln_b_ref[...])
        return task

    def act_u(r):
        def task():
            rs = slice(r * CHUNK_B, (r + 1) * CHUNK_B)
            ga_ref[rs, :] = _gelu(pa_ref[rs, 0:D_A])
        return task

    def act_z(r):
        def task():
            rs = slice(r * CHUNK_B, (r + 1) * CHUNK_B)
            ga_ref[rs, :] = ga_ref[rs, :] * _silu(pa_ref[rs, 2 * D_A:3 * D_A])
        return task

    def mixa_mix(g):
        def task():
            row = lax.broadcasted_iota(jnp.int32, (CHUNK_A, CHUNK_A), 0)
            col = lax.broadcasted_iota(jnp.int32, (CHUNK_A, CHUNK_A), 1)
            w_g = _bf(jnp.where(row >= col, w_s_ref[g], 0.0))
            cs = slice(g * GROUP_A, (g + 1) * GROUP_A)
            for c in range(N_CA):
                rs = slice(c * CHUNK_A, (c + 1) * CHUNK_A)
                mixed = _dot(w_g, vn_ref[rs, cs]) + b_s_ref[:, cs]
                a_ref[rs, cs] = _bf(ga_ref[rs, cs] * mixed)
        return task

    def decay_gate():
        logit = _dot(_bf(pb_ref[:, N_B:N_BP]), w_gu_ref[...]) + b_gu_ref[...]
        log_alpha = _log_sigmoid(logit) * (1.0 / GATE_NORM)
        g_hi = _bf(log_alpha)
        gh_ref[...] = g_hi
        gl_ref[...] = _bf(log_alpha - g_hi.astype(jnp.float32))

    def decay_cumsum():
        ri = lax.broadcasted_iota(jnp.int32, (TILE_S, TILE_S), 0)
        ci = lax.broadcasted_iota(jnp.int32, (TILE_S, TILE_S), 1)
        tril = _bf(jnp.where((ri >= ci) & (ri // CHUNK_B == ci // CHUNK_B), 1.0, 0.0))
        b_ref[...] = _dot(tril, gh_ref[...]) + _dot(tril, gl_ref[...])

    def gla_scale(c):
        def task():
            rs = slice(c * CHUNK_B, (c + 1) * CHUNK_B)
            b = b_ref[rs, :]
            b_mid = b[CHUNK_B // 2 - 1:CHUNK_B // 2]
            b_last = b[CHUNK_B - 1:CHUNK_B]
            q = pb_ref[rs, 0:KEY_B] * (DK_B ** -0.5)
            k = pb_ref[rs, KEY_B:2 * KEY_B]
            qi_ref[rs, :] = _bf(q * jnp.exp(b - b_mid))
            ki_ref[rs, :] = _bf(k * jnp.exp(b_mid - b))
            qs_ref[rs, :] = _bf(q * jnp.exp(b))
            ks_ref[rs, :] = _bf(k * jnp.exp(b_last - b))
        return task

    scores = {}

    def gla_scores(c):
        def task():
            rs = slice(c * CHUNK_B, (c + 1) * CHUNK_B)
            rc = lax.broadcasted_iota(jnp.int32, (CHUNK_B, CHUNK_B), 0)
            cc = lax.broadcasted_iota(jnp.int32, (CHUNK_B, CHUNK_B), 1)
            for h in range(H_B):
                ks = slice(h * DK_B, (h + 1) * DK_B)
                scores[c, h] = _bf(jnp.where(rc >= cc, _dot_nt(qi_ref[rs, ks], ki_ref[rs, ks]), 0.0))
        return task

    def gla_out(c):
        def task():
            rs = slice(c * CHUNK_B, (c + 1) * CHUNK_B)
            for h in range(H_B):
                ks = slice(h * DK_B, (h + 1) * DK_B)
                vs = slice(h * DV_B, (h + 1) * DV_B)
                decay = jnp.exp(b_ref[(c + 1) * CHUNK_B - 1:(c + 1) * CHUNK_B, ks])
                st = state_ref[h]
                v_h = pb_ref[rs, 2 * KEY_B + h * DV_B:2 * KEY_B + (h + 1) * DV_B]
                o_h = _dot(scores[c, h], _bf(v_h)) + _dot_nt(qs_ref[rs, ks], _bf(st))
                state_ref[h] = st * decay + _dot(_bf(v_h.T), ks_ref[rs, ks])
                z_b = pb_ref[rs, 2 * KEY_B + VAL_B + h * DV_B:2 * KEY_B + VAL_B + (h + 1) * DV_B]
                o_ref[rs, vs] = _bf(_rms(o_h, gla_g_ref[...]) * _silu(z_b))
        return task

    def gate(lo, hi):
        def task():
            gate_ref[:, lo:hi] = _sigmoid(_dot(hb_ref[...], w_g_ref[:, lo:hi]))
        return task

    def branch_a(lo, hi):
        def task():
            ba_ref[:, lo:hi] = gate_ref[:, lo:hi] * _dot(a_ref[...], w_ba_ref[:, lo:hi])
        return task

    half = TILE_S // 2

    def branch_b(r):
        def task():
            rs = slice(r * half, (r + 1) * half)
            merged = ba_ref[rs, :] + gate_ref[rs, D_MODEL:] * _dot(o_ref[rs, :], w_bb_ref[...])
            mg_ref[rs, :] = _bf(merged)
        return task

    def out_proj(r):
        def task():
            rs = slice(r * half, (r + 1) * half)
            y = x_ref[rs, :] + _dot(mg_ref[rs, :], w_out_ref[...])
            out_ref[rs, :] = _rms(y, fin_g_ref[...])
        return task

    pieces = lambda n: [(lo, min(lo + PIECE_N, n)) for lo in range(0, n, PIECE_N)]
    proj_a = [proj(pa_ref, w_a_ref, lo, hi) for lo, hi in pieces(N_A)]
    proj_qk = proj(pb_ref, w_b_ref, 0, 2 * KEY_B)
    proj_zl = proj(pb_ref, w_b_ref, 2 * KEY_B + VAL_B + PIECE_N, N_BP)
    proj_v = [proj(pb_ref, w_b_ref, 2 * KEY_B + lo, 2 * KEY_B + hi) for lo, hi in pieces(VAL_B)]
    proj_z = proj(pb_ref, w_b_ref, 2 * KEY_B + VAL_B, 2 * KEY_B + VAL_B + PIECE_N)
    gates = [gate(lo, hi) for lo, hi in pieces(N_G)]
    mix = [mixa_mix(g) for g in range(H_A)]
    rows = range(N_CB)

    order = [norm_in, proj_a[2], proj_a[3]] + [act_v(r) for r in rows]
    order += [proj_a[0], proj_a[1]] + [act_u(r) for r in rows]
    order += [proj_a[4], proj_a[5]] + [act_z(r) for r in rows]
    order += [proj_qk, proj_zl, decay_gate, proj_v[0], proj_v[1], decay_cumsum]
    order += [gla_scale(c) for c in rows]
    order += [proj_z, mix[0], mix[1]]
    for c in rows:
        order += [gla_scores(c), gates[c], gla_out(c), mix[2 + c]]
    order += [mix[6], mix[7]]
    order += [branch_a(lo, hi) for lo, hi in pieces(D_MODEL)]
    order += [branch_b(0), branch_b(1), out_proj(0), out_proj(1)]
    for task in order:
        task()


def _const_spec(shape):
    nd = len(shape)
    return pl.BlockSpec(shape, lambda b, s: (0,) * nd, pipeline_mode=pl.Buffered(1))


_SCRATCH = (
    ((H_B, DV_B, DK_B), jnp.float32),
    ((TILE_S, D_MODEL), jnp.bfloat16),
    ((TILE_S, N_A), jnp.float32),
    ((TILE_S, N_BP), jnp.float32),
    ((TILE_S, N_G), jnp.float32),
    ((TILE_S, D_A), jnp.bfloat16),
    ((TILE_S, D_A), jnp.float32),
    ((TILE_S, KEY_B), jnp.bfloat16),
    ((TILE_S, KEY_B), jnp.bfloat16),
    ((TILE_S, KEY_B), jnp.float32),
    ((TILE_S, KEY_B), jnp.bfloat16),
    ((TILE_S, KEY_B), jnp.bfloat16),
    ((TILE_S, KEY_B), jnp.bfloat16),
    ((TILE_S, KEY_B), jnp.bfloat16),
    ((TILE_S, D_MODEL), jnp.bfloat16),
    ((TILE_S, D_MODEL), jnp.bfloat16),
    ((TILE_S, D_MODEL), jnp.float32),
    ((TILE_S, D_MODEL), jnp.bfloat16),
)


def _vmem_limit_bytes():
    nbytes = lambda shape, dt: math.prod(shape) * jnp.dtype(dt).itemsize
    weights = 2 * (D_MODEL * (N_A + N_BP + N_G) + 3 * D_MODEL * D_MODEL + LR_PAD * KEY_B)
    weights += 4 * (H_A * CHUNK_A * CHUNK_A + CHUNK_A * D_A)
    io_windows = 2 * 2 * TILE_S * D_MODEL * 4
    scratch = sum(nbytes(s, d) for s, d in _SCRATCH)
    temporaries = 16 * TILE_S * D_MODEL * 4
    return min(weights + io_windows + scratch + temporaries, V7X_VMEM_BYTES - 8 * 1024 * 1024)


@jax.jit
def _forward(x, norm_g, w_in, ln_v_g, ln_v_b, w_spatial, b_spatial, w_gate_up, b_gate_up,
             gla_norm_g, w_branch_a, w_branch_b, w_out, final_norm_g):
    bsz, seq, d = x.shape
    assert d == D_MODEL and seq % TILE_S == 0
    assert TILE_S % CHUNK_A == 0 and TILE_S % CHUNK_B == 0
    bf = jnp.bfloat16
    w_a = w_in[:, :N_A].astype(bf)
    w_b = jnp.pad(w_in[:, N_A:N_A + N_B + GATE_RANK], ((0, 0), (0, LR_PAD - GATE_RANK))).astype(bf)
    w_g = w_in[:, N_A + N_B + GATE_RANK:].astype(bf)
    w_gu = jnp.pad(w_gate_up, ((0, LR_PAD - GATE_RANK), (0, 0))).astype(bf)
    b_s = jnp.repeat(b_spatial.T, GROUP_A, axis=1)
    row = lambda a: a.reshape(1, -1)

    operands = (
        x, row(norm_g), w_a, w_b, w_g, row(ln_v_g), row(ln_v_b), w_spatial, b_s,
        w_gu, row(b_gate_up), row(gla_norm_g), w_branch_a.astype(bf), w_branch_b.astype(bf),
        w_out.astype(bf), row(final_norm_g),
    )
    in_specs = [pl.BlockSpec((None, TILE_S, D_MODEL), lambda b, s: (b, s, 0))]
    in_specs += [_const_spec(op.shape) for op in operands[1:]]
    return pl.pallas_call(
        _layer_kernel,
        grid=(bsz, seq // TILE_S),
        in_specs=in_specs,
        out_specs=pl.BlockSpec((None, TILE_S, D_MODEL), lambda b, s: (b, s, 0)),
        out_shape=jax.ShapeDtypeStruct(x.shape, x.dtype),
        scratch_shapes=[pltpu.VMEM(s, d) for s, d in _SCRATCH],
        compiler_params=pltpu.CompilerParams(
            dimension_semantics=("arbitrary", "arbitrary"),
            vmem_limit_bytes=_vmem_limit_bytes(),
        ),
        name="hybrid_layer",
    )(*operands)


def kernel(x, norm_g, w_in, ln_v_g, ln_v_b, w_spatial, b_spatial, w_gate_up, b_gate_up, gla_norm_g,
           w_branch_a, w_branch_b, w_out, final_norm_g):
    assert norm_g.shape[0] == 1, "single-layer problem"
    return _forward(x, norm_g[0], w_in[0], ln_v_g[0], ln_v_b[0], w_spatial[0], b_spatial[0],
                    w_gate_up[0], b_gate_up[0], gla_norm_g[0], w_branch_a[0], w_branch_b[0],
                    w_out[0], final_norm_g)
```

```python
import math

import jax
import jax.numpy as jnp
from jax import lax
from jax.experimental import pallas as pl
from jax.experimental.pallas import tpu as pltpu

D_MODEL = 1024
D_A = D_MODEL
H_A = 8
GROUP_A = D_A // H_A
CHUNK_A = 128
H_B = 4
KEY_B = D_MODEL // 2
VAL_B = D_MODEL
DK_B = KEY_B // H_B
DV_B = VAL_B // H_B
GATE_RANK = 16
GATE_NORM = 16.0
CHUNK_B = 64
EPS = 1e-6
LN_EPS = 1e-5

V7X_LANES = 128
V7X_SUBLANES = 8
V7X_VMEM_BYTES = 64 * 1024 * 1024
TILE_S = 256
PIECE_N = 512

N_A = 3 * D_A
N_B = 2 * KEY_B + 2 * VAL_B
N_G = 2 * D_MODEL
N_CA = TILE_S // CHUNK_A
N_CB = TILE_S // CHUNK_B

COL_A = 0
COL_B = COL_A + N_A
COL_LR = COL_B + N_B
COL_G = COL_LR + V7X_LANES
W_IN_COLS = COL_G + N_G
COL_BA = 0
COL_BB = COL_BA + D_MODEL
COL_OUT = COL_BB + D_MODEL
W_SQ_COLS = COL_OUT + D_MODEL + V7X_LANES
assert (W_IN_COLS // V7X_LANES) % 2 == 1 and (W_SQ_COLS // V7X_LANES) % 2 == 1


def _bf(x):
    return x.astype(jnp.bfloat16)


def _dot(a, b):
    return jnp.dot(a, b, preferred_element_type=jnp.float32)


def _dot_nt(a, b):
    return lax.dot_general(a, b, (((1,), (1,)), ((), ())), preferred_element_type=jnp.float32)


def _gelu(x):
    return 0.5 * x * (1.0 + lax.erf(x * (1.0 / math.sqrt(2.0))))


def _sigmoid(x):
    return 1.0 / (1.0 + jnp.exp(-x))


def _silu(x):
    return x * _sigmoid(x)


def _log_sigmoid(x):
    return jnp.minimum(x, 0.0) - jnp.log1p(jnp.exp(-jnp.abs(x)))


def _rms(x, g):
    return x * lax.rsqrt(jnp.mean(x * x, axis=-1, keepdims=True) + EPS) * g


def _layer_kernel(x_ref, norm_g_ref, w_in_ref, w_sq_ref, ln_g_ref, ln_b_ref, w_s_ref, b_s_ref,
                  w_gu_ref, b_gu_ref, gla_g_ref, fin_g_ref,
                  out_ref,
                  state_ref, hb_ref, pa_ref, pb_ref, gate_ref, vn_ref, ga_ref, gh_ref, gl_ref, b_ref,
                  qi_ref, ki_ref, qs_ref, ks_ref, a_ref, o_ref, ba_ref, mg_ref):
    @pl.when(pl.program_id(1) == 0)
    def _():
        state_ref[...] = jnp.zeros_like(state_ref)

    def norm_in():
        hb_ref[...] = _bf(_rms(x_ref[...], norm_g_ref[...]))

    def proj(dst_ref, col0, lo, hi):
        def task():
            dst_ref[:, lo:hi] = _dot(hb_ref[...], w_in_ref[:, col0 + lo:col0 + hi])
        return task

    def act_v(r):
        def task():
            rs = slice(r * CHUNK_B, (r + 1) * CHUNK_B)
            v = _gelu(pa_ref[rs, D_A:2 * D_A])
            mu = jnp.mean(v, axis=-1, keepdims=True)
            vc = v - mu
            inv = lax.rsqrt(jnp.mean(vc * vc, axis=-1, keepdims=True) + LN_EPS)
            vn_ref[rs, :] = _bf(vc * inv * ln_g_ref[...] + ln_b_ref[...])
        return task

    def act_u(r):
        def task():
            rs = slice(r * CHUNK_B, (r + 1) * CHUNK_B)
            ga_ref[rs, :] = _gelu(pa_ref[rs, 0:D_A])
        return task

    def act_z(r):
        def task():
            rs = slice(r * CHUNK_B, (r + 1) * CHUNK_B)
            ga_ref[rs, :] = ga_ref[rs, :] * _silu(pa_ref[rs, 2 * D_A:3 * D_A])
        return task

    def mixa_mix(g):
        def task():
            row = lax.broadcasted_iota(jnp.int32, (CHUNK_A, CHUNK_A), 0)
            col = lax.broadcasted_iota(jnp.int32, (CHUNK_A, CHUNK_A), 1)
            w_g = _bf(jnp.where(row >= col, w_s_ref[g], 0.0))
            cs = slice(g * GROUP_A, (g + 1) * GROUP_A)
            chunks = [slice(c * CHUNK_A, (c + 1) * CHUNK_A) for c in range(N_CA)]
            mixed = _dot(w_g, jnp.concatenate([vn_ref[rs, cs] for rs in chunks], axis=1))
            for c, rs in enumerate(chunks):
                m_c = mixed[:, c * GROUP_A:(c + 1) * GROUP_A] + b_s_ref[:, cs]
                a_ref[rs, cs] = _bf(ga_ref[rs, cs] * m_c)
        return task

    def decay_gate():
        logit = _dot(_bf(pb_ref[:, N_B:N_B + V7X_LANES]), w_gu_ref[...]) + b_gu_ref[...]
        log_alpha = _log_sigmoid(logit) * (1.0 / GATE_NORM)
        g_hi = _bf(log_alpha)
        gh_ref[...] = g_hi
        gl_ref[...] = _bf(log_alpha - g_hi.astype(jnp.float32))

    def decay_cumsum():
        ri = lax.broadcasted_iota(jnp.int32, (TILE_S, TILE_S), 0)
        ci = lax.broadcasted_iota(jnp.int32, (TILE_S, TILE_S), 1)
        tril = _bf(jnp.where((ri >= ci) & (ri // CHUNK_B == ci // CHUNK_B), 1.0, 0.0))
        b_ref[...] = _dot(tril, gh_ref[...]) + _dot(tril, gl_ref[...])

    def gla_scale(c):
        def task():
            rs = slice(c * CHUNK_B, (c + 1) * CHUNK_B)
            b = b_ref[rs, :]
            b_mid = b[CHUNK_B // 2 - 1:CHUNK_B // 2]
            b_last = b[CHUNK_B - 1:CHUNK_B]
            q = pb_ref[rs, 0:KEY_B] * (DK_B ** -0.5)
            k = pb_ref[rs, KEY_B:2 * KEY_B]
            qi_ref[rs, :] = _bf(q * jnp.exp(b - b_mid))
            ki_ref[rs, :] = _bf(k * jnp.exp(b_mid - b))
            qs_ref[rs, :] = _bf(q * jnp.exp(b))
            ks_ref[rs, :] = k * jnp.exp(b_last - b)
        return task

    scores = {}

    def gla_scores(c):
        def task():
            rs = slice(c * CHUNK_B, (c + 1) * CHUNK_B)
            rc = lax.broadcasted_iota(jnp.int32, (CHUNK_B, CHUNK_B), 0)
            cc = lax.broadcasted_iota(jnp.int32, (CHUNK_B, CHUNK_B), 1)
            for h in range(H_B):
                ks = slice(h * DK_B, (h + 1) * DK_B)
                scores[c, h] = _bf(jnp.where(rc >= cc, _dot_nt(qi_ref[rs, ks], ki_ref[rs, ks]), 0.0))
        return task

    def gla_out(c):
        def task():
            rs = slice(c * CHUNK_B, (c + 1) * CHUNK_B)
            for h in range(H_B):
                ks = slice(h * DK_B, (h + 1) * DK_B)
                vs = slice(h * DV_B, (h + 1) * DV_B)
                st = state_ref[h]
                v_h = _bf(pb_ref[rs, 2 * KEY_B + h * DV_B:2 * KEY_B + (h + 1) * DV_B])
                o_h = _dot(scores[c, h], v_h) + _dot(qs_ref[rs, ks], _bf(st))
                decay_row = jnp.exp(b_ref[(c + 1) * CHUNK_B - 1:(c + 1) * CHUNK_B, ks])
                decay_col = jnp.broadcast_to(decay_row, (V7X_SUBLANES, DK_B)).T[:, 0:1]
                state_ref[h] = st * decay_col + _dot(_bf(ks_ref[rs, ks].T), v_h)
                z_b = pb_ref[rs, 2 * KEY_B + VAL_B + h * DV_B:2 * KEY_B + VAL_B + (h + 1) * DV_B]
                o_ref[rs, vs] = _bf(_rms(o_h, gla_g_ref[...]) * _silu(z_b))
        return task

    def gate(lo, hi):
        def task():
            gate_ref[:, lo:hi] = _sigmoid(_dot(hb_ref[...], w_in_ref[:, COL_G + lo:COL_G + hi]))
        return task

    def branch_a(lo, hi):
        def task():
            ba_ref[:, lo:hi] = gate_ref[:, lo:hi] * _dot(a_ref[...], w_sq_ref[:, COL_BA + lo:COL_BA + hi])
        return task

    half = TILE_S // 2

    def branch_b(r):
        def task():
            rs = slice(r * half, (r + 1) * half)
            p_b = _dot(o_ref[rs, :], w_sq_ref[:, COL_BB:COL_BB + D_MODEL])
            mg_ref[rs, :] = _bf(ba_ref[rs, :] + gate_ref[rs, D_MODEL:] * p_b)
        return task

    def out_proj(r):
        def task():
            rs = slice(r * half, (r + 1) * half)
            y = x_ref[rs, :] + _dot(mg_ref[rs, :], w_sq_ref[:, COL_OUT:COL_OUT + D_MODEL])
            out_ref[rs, :] = _rms(y, fin_g_ref[...])
        return task

    pieces = lambda n: [(lo, min(lo + PIECE_N, n)) for lo in range(0, n, PIECE_N)]
    proj_a = [proj(pa_ref, COL_A, lo, hi) for lo, hi in pieces(N_A)]
    proj_qk = proj(pb_ref, COL_B, 0, 2 * KEY_B)
    proj_zl = proj(pb_ref, COL_B, 2 * KEY_B + VAL_B + PIECE_N, N_B + V7X_LANES)
    proj_v = [proj(pb_ref, COL_B, 2 * KEY_B + lo, 2 * KEY_B + hi) for lo, hi in pieces(VAL_B)]
    proj_z = proj(pb_ref, COL_B, 2 * KEY_B + VAL_B, 2 * KEY_B + VAL_B + PIECE_N)
    gates = [gate(lo, hi) for lo, hi in pieces(N_G)]
    mix = [mixa_mix(g) for g in range(H_A)]
    rows = range(N_CB)

    order = [norm_in, proj_a[2], proj_a[3]] + [act_v(r) for r in rows]
    order += [proj_a[0], proj_a[1]] + [act_u(r) for r in rows]
    order += [proj_a[4], proj_a[5]] + [act_z(r) for r in rows]
    order += [proj_qk, proj_zl, decay_gate, proj_v[0], proj_v[1], decay_cumsum]
    order += [gla_scale(c) for c in rows]
    order += [proj_z, mix[0], mix[1]]
    for c in rows:
        order += [gla_scores(c), gates[c], gla_out(c), mix[2 + c]]
    order += [mix[6], mix[7]]
    order += [branch_a(lo, hi) for lo, hi in pieces(D_MODEL)]
    order += [branch_b(0), branch_b(1), out_proj(0), out_proj(1)]
    for task in order:
        task()


def _const_spec(shape):
    nd = len(shape)
    return pl.BlockSpec(shape, lambda b, s: (0,) * nd, pipeline_mode=pl.Buffered(1))


_SCRATCH = (
    ((H_B, DK_B, DV_B), jnp.float32),
    ((TILE_S, D_MODEL), jnp.bfloat16),
    ((TILE_S, N_A), jnp.float32),
    ((TILE_S, N_B + V7X_LANES), jnp.float32),
    ((TILE_S, N_G), jnp.float32),
    ((TILE_S, D_A), jnp.bfloat16),
    ((TILE_S, D_A), jnp.float32),
    ((TILE_S, KEY_B), jnp.bfloat16),
    ((TILE_S, KEY_B), jnp.bfloat16),
    ((TILE_S, KEY_B), jnp.float32),
    ((TILE_S, KEY_B), jnp.bfloat16),
    ((TILE_S, KEY_B), jnp.bfloat16),
    ((TILE_S, KEY_B), jnp.bfloat16),
    ((TILE_S, KEY_B), jnp.float32),
    ((TILE_S, D_MODEL), jnp.bfloat16),
    ((TILE_S, D_MODEL), jnp.bfloat16),
    ((TILE_S, D_MODEL), jnp.float32),
    ((TILE_S, D_MODEL), jnp.bfloat16),
)


def _vmem_limit_bytes():
    nbytes = lambda shape, dt: math.prod(shape) * jnp.dtype(dt).itemsize
    weights = 2 * D_MODEL * (W_IN_COLS + W_SQ_COLS) + 2 * V7X_LANES * KEY_B
    weights += 4 * (H_A * CHUNK_A * CHUNK_A + CHUNK_A * D_A)
    io_windows = 2 * 2 * TILE_S * D_MODEL * 4
    scratch = sum(nbytes(s, d) for s, d in _SCRATCH)
    temporaries = 16 * TILE_S * D_MODEL * 4
    return min(weights + io_windows + scratch + temporaries, V7X_VMEM_BYTES - 8 * 1024 * 1024)


@jax.jit
def _forward(x, norm_g, w_in, ln_v_g, ln_v_b, w_spatial, b_spatial, w_gate_up, b_gate_up,
             gla_norm_g, w_branch_a, w_branch_b, w_out, final_norm_g):
    bsz, seq, d = x.shape
    assert d == D_MODEL and seq % TILE_S == 0
    assert TILE_S % CHUNK_A == 0 and TILE_S % CHUNK_B == 0
    bf = jnp.bfloat16
    lr_end = N_A + N_B + GATE_RANK
    zeros = lambda n: jnp.zeros((D_MODEL, n), w_in.dtype)
    w_in_slab = jnp.concatenate(
        [w_in[:, :lr_end], zeros(V7X_LANES - GATE_RANK), w_in[:, lr_end:]], axis=1).astype(bf)
    w_sq_slab = jnp.concatenate([w_branch_a, w_branch_b, w_out, zeros(V7X_LANES)], axis=1).astype(bf)
    w_gu = jnp.pad(w_gate_up, ((0, V7X_LANES - GATE_RANK), (0, 0))).astype(bf)
    b_s = jnp.repeat(b_spatial.T, GROUP_A, axis=1)
    row = lambda a: a.reshape(1, -1)

    operands = (
        x, row(norm_g), w_in_slab, w_sq_slab, row(ln_v_g), row(ln_v_b), w_spatial, b_s,
        w_gu, row(b_gate_up), row(gla_norm_g), row(final_norm_g),
    )
    in_specs = [pl.BlockSpec((None, TILE_S, D_MODEL), lambda b, s: (b, s, 0))]
    in_specs += [_const_spec(op.shape) for op in operands[1:]]
    return pl.pallas_call(
        _layer_kernel,
        grid=(bsz, seq // TILE_S),
        in_specs=in_specs,
        out_specs=pl.BlockSpec((None, TILE_S, D_MODEL), lambda b, s: (b, s, 0)),
        out_shape=jax.ShapeDtypeStruct(x.shape, x.dtype),
        scratch_shapes=[pltpu.VMEM(s, d) for s, d in _SCRATCH],
        compiler_params=pltpu.CompilerParams(
            dimension_semantics=("arbitrary", "arbitrary"),
            vmem_limit_bytes=_vmem_limit_bytes(),
        ),
        name="hybrid_layer",
    )(*operands)


def kernel(x, norm_g, w_in, ln_v_g, ln_v_b, w_spatial, b_spatial, w_gate_up, b_gate_up, gla_norm_g,
           w_branch_a, w_branch_b, w_out, final_norm_g):
    assert norm_g.shape[0] == 1, "single-layer problem"
    return _forward(x, norm_g[0], w_in[0], ln_v_g[0], ln_v_b[0], w_spatial[0], b_spatial[0],
                    w_gate_up[0], b_gate_up[0], gla_norm_g[0], w_branch_a[0], w_branch_b[0],
                    w_out[0], final_norm_g)
```

```python
import math

import jax
import jax.numpy as jnp
from jax import lax
from jax.experimental import pallas as pl
from jax.experimental.pallas import tpu as pltpu

D_MODEL = 1024
D_A = D_MODEL
H_A = 8
GROUP_A = D_A // H_A
CHUNK_A = 128
H_B = 4
KEY_B = D_MODEL // 2
VAL_B = D_MODEL
DK_B = KEY_B // H_B
DV_B = VAL_B // H_B
GATE_RANK = 16
GATE_NORM = 16.0
CHUNK_B = 64
EPS = 1e-6
LN_EPS = 1e-5

V7X_LANES = 128
V7X_SUBLANES = 8
V7X_VMEM_BYTES = 64 * 1024 * 1024
TILE_S = 256
PIECE_N = 512
STAGE_ROWS = 64

N_A = 3 * D_A
N_B = 2 * KEY_B + 2 * VAL_B
N_G = 2 * D_MODEL
N_CA = TILE_S // CHUNK_A
N_CB = TILE_S // CHUNK_B

COL_A = 0
COL_B = COL_A + N_A
COL_LR = COL_B + N_B
COL_G = COL_LR + V7X_LANES
W_IN_COLS = COL_G + N_G
COL_BA = 0
COL_BB = COL_BA + D_MODEL
COL_OUT = COL_BB + D_MODEL
W_SQ_COLS = COL_OUT + D_MODEL + V7X_LANES
assert (W_IN_COLS // V7X_LANES) % 2 == 1 and (W_SQ_COLS // V7X_LANES) % 2 == 1


def _bf(x):
    return x.astype(jnp.bfloat16)


def _dot(a, b):
    return jnp.dot(a, b, preferred_element_type=jnp.float32)


def _dot_nt(a, b):
    return lax.dot_general(a, b, (((1,), (1,)), ((), ())), preferred_element_type=jnp.float32)


def _gelu(x):
    return 0.5 * x * (1.0 + lax.erf(x * (1.0 / math.sqrt(2.0))))


def _sigmoid(x):
    return 1.0 / (1.0 + jnp.exp(-x))


def _silu(x):
    return x * _sigmoid(x)


def _log_sigmoid(x):
    return jnp.minimum(x, 0.0) - jnp.log1p(jnp.exp(-jnp.abs(x)))


def _rms(x, g):
    return x * lax.rsqrt(jnp.mean(x * x, axis=-1, keepdims=True) + EPS) * g


def _stage_weights(w_in_hbm, w_g_hbm, w_sq_hbm, w_in_ref, w_sq_ref, st_in_ref, st_g_ref, st_sq_ref, sem_ref):
    n_chunks = D_MODEL // STAGE_ROWS

    def copies(i, slot):
        rows = pl.ds(pl.multiple_of(i * STAGE_ROWS, STAGE_ROWS), STAGE_ROWS)
        out = [pltpu.make_async_copy(w_in_hbm.at[rows, pl.ds(0, COL_G)], st_in_ref.at[slot], sem_ref.at[slot, 0]),
               pltpu.make_async_copy(w_g_hbm.at[rows, :], st_g_ref.at[slot], sem_ref.at[slot, 1])]
        out += [pltpu.make_async_copy(w.at[rows, :], st_sq_ref.at[slot, k], sem_ref.at[slot, 2 + k])
                for k, w in enumerate(w_sq_hbm)]
        return out

    for cp in copies(0, 0):
        cp.start()

    def body(i, carry):
        slot = lax.rem(i, 2)

        @pl.when(i + 1 < n_chunks)
        def _():
            for cp in copies(i + 1, 1 - slot):
                cp.start()

        for cp in copies(i, slot):
            cp.wait()
        rows = pl.ds(pl.multiple_of(i * STAGE_ROWS, STAGE_ROWS), STAGE_ROWS)
        w_in_ref[rows, 0:COL_G] = _bf(st_in_ref[slot])
        w_in_ref[rows, COL_G:W_IN_COLS] = _bf(st_g_ref[slot])
        for k in range(len(w_sq_hbm)):
            w_sq_ref[rows, k * D_MODEL:(k + 1) * D_MODEL] = _bf(st_sq_ref[slot, k])
        return carry

    lax.fori_loop(0, n_chunks, body, 0)


def _layer_kernel(x_ref, norm_g_ref, w_in_hbm, w_g_hbm, w_ba_hbm, w_bb_hbm, w_out_hbm,
                  ln_g_ref, ln_b_ref, w_s_ref, b_s_ref, w_gu_ref, b_gu_ref, gla_g_ref, fin_g_ref,
                  out_ref,
                  w_in_ref, w_sq_ref, st_in_ref, st_g_ref, st_sq_ref, sem_ref,
                  state_ref, hb_ref, pa_ref, pb_ref, gate_ref, vn_ref, ga_ref, gh_ref, gl_ref, b_ref,
                  qi_ref, ki_ref, qs_ref, ks_ref, a_ref, o_ref, ba_ref, mg_ref):
    @pl.when((pl.program_id(0) == 0) & (pl.program_id(1) == 0))
    def _():
        _stage_weights(w_in_hbm, w_g_hbm, (w_ba_hbm, w_bb_hbm, w_out_hbm), w_in_ref, w_sq_ref,
                       st_in_ref, st_g_ref, st_sq_ref, sem_ref)

    @pl.when(pl.program_id(1) == 0)
    def _():
        state_ref[...] = jnp.zeros_like(state_ref)

    def norm_in():
        hb_ref[...] = _bf(_rms(x_ref[...], norm_g_ref[...]))

    def proj(dst_ref, col0, lo, hi):
        def task():
            dst_ref[:, lo:hi] = _dot(hb_ref[...], w_in_ref[:, col0 + lo:col0 + hi])
        return task

    def act_v(r):
        def task():
            rs = slice(r * CHUNK_B, (r + 1) * CHUNK_B)
            v = _gelu(pa_ref[rs, D_A:2 * D_A])
            mu = jnp.mean(v, axis=-1, keepdims=True)
            vc = v - mu
            inv = lax.rsqrt(jnp.mean(vc * vc, axis=-1, keepdims=True) + LN_EPS)
            vn_ref[rs, :] = _bf(vc * inv * ln_g_ref[...] + ln_b_ref[...])
        return task

    def act_u(r):
        def task():
            rs = slice(r * CHUNK_B, (r + 1) * CHUNK_B)
            ga_ref[rs, :] = _gelu(pa_ref[rs, 0:D_A])
        return task

    def act_z(r):
        def task():
            rs = slice(r * CHUNK_B, (r + 1) * CHUNK_B)
            ga_ref[rs, :] = ga_ref[rs, :] * _silu(pa_ref[rs, 2 * D_A:3 * D_A])
        return task

    def mixa_mix(g):
        def task():
            row = lax.broadcasted_iota(jnp.int32, (CHUNK_A, CHUNK_A), 0)
            col = lax.broadcasted_iota(jnp.int32, (CHUNK_A, CHUNK_A), 1)
            w_g = _bf(jnp.where(row >= col, w_s_ref[g], 0.0))
            cs = slice(g * GROUP_A, (g + 1) * GROUP_A)
            chunks = [slice(c * CHUNK_A, (c + 1) * CHUNK_A) for c in range(N_CA)]
            mixed = _dot(w_g, jnp.concatenate([vn_ref[rs, cs] for rs in chunks], axis=1))
            for c, rs in enumerate(chunks):
                m_c = mixed[:, c * GROUP_A:(c + 1) * GROUP_A] + b_s_ref[:, cs]
                a_ref[rs, cs] = _bf(ga_ref[rs, cs] * m_c)
        return task

    def decay_gate():
        logit = _dot(_bf(pb_ref[:, N_B:N_B + V7X_LANES]), w_gu_ref[...]) + b_gu_ref[...]
        log_alpha = _log_sigmoid(logit) * (1.0 / GATE_NORM)
        g_hi = _bf(log_alpha)
        gh_ref[...] = g_hi
        gl_ref[...] = _bf(log_alpha - g_hi.astype(jnp.float32))

    def decay_cumsum():
        ri = lax.broadcasted_iota(jnp.int32, (TILE_S, TILE_S), 0)
        ci = lax.broadcasted_iota(jnp.int32, (TILE_S, TILE_S), 1)
        tril = _bf(jnp.where((ri >= ci) & (ri // CHUNK_B == ci // CHUNK_B), 1.0, 0.0))
        b_ref[...] = _dot(tril, gh_ref[...]) + _dot(tril, gl_ref[...])

    def gla_scale(c):
        def task():
            rs = slice(c * CHUNK_B, (c + 1) * CHUNK_B)
            b = b_ref[rs, :]
            b_mid = b[CHUNK_B // 2 - 1:CHUNK_B // 2]
            b_last = b[CHUNK_B - 1:CHUNK_B]
            q = pb_ref[rs, 0:KEY_B] * (DK_B ** -0.5)
            k = pb_ref[rs, KEY_B:2 * KEY_B]
            qi_ref[rs, :] = _bf(q * jnp.exp(b - b_mid))
            ki_ref[rs, :] = _bf(k * jnp.exp(b_mid - b))
            qs_ref[rs, :] = _bf(q * jnp.exp(b))
            ks_ref[rs, :] = k * jnp.exp(b_last - b)
        return task

    scores = {}

    def gla_scores(c):
        def task():
            rs = slice(c * CHUNK_B, (c + 1) * CHUNK_B)
            rc = lax.broadcasted_iota(jnp.int32, (CHUNK_B, CHUNK_B), 0)
            cc = lax.broadcasted_iota(jnp.int32, (CHUNK_B, CHUNK_B), 1)
            for h in range(H_B):
                ks = slice(h * DK_B, (h + 1) * DK_B)
                scores[c, h] = _bf(jnp.where(rc >= cc, _dot_nt(qi_ref[rs, ks], ki_ref[rs, ks]), 0.0))
        return task

    def gla_out(c):
        def task():
            rs = slice(c * CHUNK_B, (c + 1) * CHUNK_B)
            for h in range(H_B):
                ks = slice(h * DK_B, (h + 1) * DK_B)
                vs = slice(h * DV_B, (h + 1) * DV_B)
                st = state_ref[h]
                v_h = _bf(pb_ref[rs, 2 * KEY_B + h * DV_B:2 * KEY_B + (h + 1) * DV_B])
                o_h = _dot(scores[c, h], v_h) + _dot(qs_ref[rs, ks], _bf(st))
                decay_row = jnp.exp(b_ref[(c + 1) * CHUNK_B - 1:(c + 1) * CHUNK_B, ks])
                decay_col = jnp.broadcast_to(decay_row, (V7X_SUBLANES, DK_B)).T[:, 0:1]
                state_ref[h] = st * decay_col + _dot(_bf(ks_ref[rs, ks].T), v_h)
                z_b = pb_ref[rs, 2 * KEY_B + VAL_B + h * DV_B:2 * KEY_B + VAL_B + (h + 1) * DV_B]
                o_ref[rs, vs] = _bf(_rms(o_h, gla_g_ref[...]) * _silu(z_b))
        return task

    def gate(lo, hi):
        def task():
            gate_ref[:, lo:hi] = _sigmoid(_dot(hb_ref[...], w_in_ref[:, COL_G + lo:COL_G + hi]))
        return task

    def branch_a(lo, hi):
        def task():
            ba_ref[:, lo:hi] = gate_ref[:, lo:hi] * _dot(a_ref[...], w_sq_ref[:, COL_BA + lo:COL_BA + hi])
        return task

    half = TILE_S // 2

    def branch_b(r):
        def task():
            rs = slice(r * half, (r + 1) * half)
            p_b = _dot(o_ref[rs, :], w_sq_ref[:, COL_BB:COL_BB + D_MODEL])
            mg_ref[rs, :] = _bf(ba_ref[rs, :] + gate_ref[rs, D_MODEL:] * p_b)
        return task

    def out_proj(r):
        def task():
            rs = slice(r * half, (r + 1) * half)
            y = x_ref[rs, :] + _dot(mg_ref[rs, :], w_sq_ref[:, COL_OUT:COL_OUT + D_MODEL])
            out_ref[rs, :] = _rms(y, fin_g_ref[...])
        return task

    pieces = lambda n: [(lo, min(lo + PIECE_N, n)) for lo in range(0, n, PIECE_N)]
    proj_a = [proj(pa_ref, COL_A, lo, hi) for lo, hi in pieces(N_A)]
    proj_qk = proj(pb_ref, COL_B, 0, 2 * KEY_B)
    proj_zl = proj(pb_ref, COL_B, 2 * KEY_B + VAL_B + PIECE_N, N_B + V7X_LANES)
    proj_v = [proj(pb_ref, COL_B, 2 * KEY_B + lo, 2 * KEY_B + hi) for lo, hi in pieces(VAL_B)]
    proj_z = proj(pb_ref, COL_B, 2 * KEY_B + VAL_B, 2 * KEY_B + VAL_B + PIECE_N)
    gates = [gate(lo, hi) for lo, hi in pieces(N_G)]
    mix = [mixa_mix(g) for g in range(H_A)]
    rows = range(N_CB)

    order = [norm_in, proj_a[2], proj_a[3]] + [act_v(r) for r in rows]
    order += [proj_a[0], proj_a[1]] + [act_u(r) for r in rows]
    order += [proj_a[4], proj_a[5]] + [act_z(r) for r in rows]
    order += [proj_qk, proj_zl, decay_gate, proj_v[0], proj_v[1], decay_cumsum]
    order += [gla_scale(c) for c in rows]
    order += [proj_z, mix[0], mix[1]]
    for c in rows:
        order += [gla_scores(c), gates[c], gla_out(c), mix[2 + c]]
    order += [mix[6], mix[7]]
    order += [branch_a(lo, hi) for lo, hi in pieces(D_MODEL)]
    order += [branch_b(0), branch_b(1), out_proj(0), out_proj(1)]
    for task in order:
        task()


def _const_spec(shape):
    nd = len(shape)
    return pl.BlockSpec(shape, lambda b, s: (0,) * nd, pipeline_mode=pl.Buffered(1))


N_STAGE_SLOTS = 2
N_WEIGHT_STREAMS = 5

_WEIGHT_SCRATCH = (
    ((D_MODEL, W_IN_COLS), jnp.bfloat16),
    ((D_MODEL, W_SQ_COLS), jnp.bfloat16),
    ((N_STAGE_SLOTS, STAGE_ROWS, COL_G), jnp.float32),
    ((N_STAGE_SLOTS, STAGE_ROWS, N_G), jnp.float32),
    ((N_STAGE_SLOTS, 3, STAGE_ROWS, D_MODEL), jnp.float32),
)

_SCRATCH = (
    ((H_B, DK_B, DV_B), jnp.float32),
    ((TILE_S, D_MODEL), jnp.bfloat16),
    ((TILE_S, N_A), jnp.float32),
    ((TILE_S, N_B + V7X_LANES), jnp.float32),
    ((TILE_S, N_G), jnp.float32),
    ((TILE_S, D_A), jnp.bfloat16),
    ((TILE_S, D_A), jnp.float32),
    ((TILE_S, KEY_B), jnp.bfloat16),
    ((TILE_S, KEY_B), jnp.bfloat16),
    ((TILE_S, KEY_B), jnp.float32),
    ((TILE_S, KEY_B), jnp.bfloat16),
    ((TILE_S, KEY_B), jnp.bfloat16),
    ((TILE_S, KEY_B), jnp.bfloat16),
    ((TILE_S, KEY_B), jnp.float32),
    ((TILE_S, D_MODEL), jnp.bfloat16),
    ((TILE_S, D_MODEL), jnp.bfloat16),
    ((TILE_S, D_MODEL), jnp.float32),
    ((TILE_S, D_MODEL), jnp.bfloat16),
)


def _vmem_limit_bytes():
    nbytes = lambda shape, dt: math.prod(shape) * jnp.dtype(dt).itemsize
    params = 2 * V7X_LANES * KEY_B + 4 * (H_A * CHUNK_A * CHUNK_A + CHUNK_A * D_A)
    io_windows = 2 * 2 * TILE_S * D_MODEL * 4
    scratch = sum(nbytes(s, d) for s, d in _WEIGHT_SCRATCH + _SCRATCH)
    temporaries = 12 * TILE_S * D_MODEL * 4
    return min(params + io_windows + scratch + temporaries, V7X_VMEM_BYTES - 8 * 1024 * 1024)


@jax.jit
def _forward(x, norm_g, w_in, ln_v_g, ln_v_b, w_spatial, b_spatial, w_gate_up, b_gate_up,
             gla_norm_g, w_branch_a, w_branch_b, w_out, final_norm_g):
    bsz, seq, d = x.shape
    assert d == D_MODEL and seq % TILE_S == 0
    assert TILE_S % CHUNK_A == 0 and TILE_S % CHUNK_B == 0
    assert w_in.shape == (D_MODEL, N_A + N_B + GATE_RANK + N_G)
    w_g = w_in[:, N_A + N_B + GATE_RANK:]
    w_gu = jnp.pad(w_gate_up, ((0, V7X_LANES - GATE_RANK), (0, 0))).astype(jnp.bfloat16)
    b_s = jnp.repeat(b_spatial.T, GROUP_A, axis=1)
    row = lambda a: a.reshape(1, -1)

    hbm_weights = (w_in, w_g, w_branch_a, w_branch_b, w_out)
    params = (row(ln_v_g), row(ln_v_b), w_spatial, b_s, w_gu, row(b_gate_up), row(gla_norm_g),
              row(final_norm_g))
    operands = (x, row(norm_g)) + hbm_weights + params
    in_specs = [pl.BlockSpec((None, TILE_S, D_MODEL), lambda b, s: (b, s, 0)), _const_spec((1, D_MODEL))]
    in_specs += [pl.BlockSpec(memory_space=pl.ANY) for _ in hbm_weights]
    in_specs += [_const_spec(op.shape) for op in params]
    return pl.pallas_call(
        _layer_kernel,
        grid=(bsz, seq // TILE_S),
        in_specs=in_specs,
        out_specs=pl.BlockSpec((None, TILE_S, D_MODEL), lambda b, s: (b, s, 0)),
        out_shape=jax.ShapeDtypeStruct(x.shape, x.dtype),
        scratch_shapes=[pltpu.VMEM(s, d) for s, d in _WEIGHT_SCRATCH]
        + [pltpu.SemaphoreType.DMA((N_STAGE_SLOTS, N_WEIGHT_STREAMS))]
        + [pltpu.VMEM(s, d) for s, d in _SCRATCH],
        compiler_params=pltpu.CompilerParams(
            dimension_semantics=("arbitrary", "arbitrary"),
            vmem_limit_bytes=_vmem_limit_bytes(),
        ),
        name="hybrid_layer",
    )(*operands)


def kernel(x, norm_g, w_in, ln_v_g, ln_v_b, w_spatial, b_spatial, w_gate_up, b_gate_up, gla_norm_g,
           w_branch_a, w_branch_b, w_out, final_norm_g):
    assert norm_g.shape[0] == 1, "single-layer problem"
    return _forward(x, norm_g[0], w_in[0], ln_v_g[0], ln_v_b[0], w_spatial[0], b_spatial[0],
                    w_gate_up[0], b_gate_up[0], gla_norm_g[0], w_branch_a[0], w_branch_b[0],
                    w_out[0], final_norm_g)
```

```python
import math

import jax
import jax.numpy as jnp
from jax import lax
from jax.experimental import pallas as pl
from jax.experimental.pallas import tpu as pltpu

D_MODEL = 1024
D_A = D_MODEL
H_A = 8
GROUP_A = D_A // H_A
CHUNK_A = 128
H_B = 4
KEY_B = D_MODEL // 2
VAL_B = D_MODEL
DK_B = KEY_B // H_B
DV_B = VAL_B // H_B
GATE_RANK = 16
GATE_NORM = 16.0
CHUNK_B = 64
EPS = 1e-6
LN_EPS = 1e-5

V7X_LANES = 128
V7X_SUBLANES = 8
V7X_VMEM_BYTES = 64 * 1024 * 1024
TILE_S = 256
PIECE_N = 512
STAGE_ROWS = 64

N_A = 3 * D_A
N_B = 2 * KEY_B + 2 * VAL_B
N_G = 2 * D_MODEL
N_CA = TILE_S // CHUNK_A
N_CB = TILE_S // CHUNK_B

COL_A = 0
COL_B = COL_A + N_A
COL_LR = COL_B + N_B
COL_G = COL_LR + V7X_LANES
W_IN_COLS = COL_G + N_G
COL_BA = 0
COL_BB = COL_BA + D_MODEL
COL_OUT = COL_BB + D_MODEL
W_SQ_COLS = COL_OUT + D_MODEL + V7X_LANES
assert (W_IN_COLS // V7X_LANES) % 2 == 1 and (W_SQ_COLS // V7X_LANES) % 2 == 1


def _bf(x):
    return x.astype(jnp.bfloat16)


def _dot(a, b):
    return jnp.dot(a, b, preferred_element_type=jnp.float32)


def _dot_nt(a, b):
    return lax.dot_general(a, b, (((1,), (1,)), ((), ())), preferred_element_type=jnp.float32)


def _gelu(x):
    return 0.5 * x * (1.0 + lax.erf(x * (1.0 / math.sqrt(2.0))))


def _sigmoid(x):
    return 1.0 / (1.0 + jnp.exp(-x))


def _silu(x):
    return x * _sigmoid(x)


def _log_sigmoid(x):
    return jnp.minimum(x, 0.0) - jnp.log1p(jnp.exp(-jnp.abs(x)))


def _rms(x, g):
    return x * lax.rsqrt(jnp.mean(x * x, axis=-1, keepdims=True) + EPS) * g


def _stage_weights(w_in_hbm, w_sq_hbm, w_in_ref, w_sq_ref, st_in_ref, st_sq_ref, sem_ref):
    n_chunks = D_MODEL // STAGE_ROWS
    gate0 = N_A + N_B + GATE_RANK

    def copies(i, slot):
        rows = pl.ds(pl.multiple_of(i * STAGE_ROWS, STAGE_ROWS), STAGE_ROWS)
        out = [pltpu.make_async_copy(w_in_hbm.at[rows, :], st_in_ref.at[slot], sem_ref.at[slot, 0])]
        out += [pltpu.make_async_copy(w.at[rows, :], st_sq_ref.at[slot, k], sem_ref.at[slot, 1 + k])
                for k, w in enumerate(w_sq_hbm)]
        return out

    for cp in copies(0, 0):
        cp.start()

    def body(i, carry):
        slot = lax.rem(i, 2)

        @pl.when(i + 1 < n_chunks)
        def _():
            for cp in copies(i + 1, 1 - slot):
                cp.start()

        for cp in copies(i, slot):
            cp.wait()
        rows = pl.ds(pl.multiple_of(i * STAGE_ROWS, STAGE_ROWS), STAGE_ROWS)
        w_in_ref[rows, 0:COL_G] = _bf(st_in_ref[slot, :, 0:COL_G])
        w_in_ref[rows, COL_G:W_IN_COLS] = _bf(st_in_ref[slot, :, gate0:gate0 + N_G])
        for k in range(len(w_sq_hbm)):
            w_sq_ref[rows, k * D_MODEL:(k + 1) * D_MODEL] = _bf(st_sq_ref[slot, k])
        return carry

    lax.fori_loop(0, n_chunks, body, 0)


def _layer_kernel(x_ref, norm_g_ref, w_in_hbm, w_ba_hbm, w_bb_hbm, w_out_hbm,
                  ln_g_ref, ln_b_ref, w_s_ref, b_s_ref, w_gu_ref, b_gu_ref, gla_g_ref, fin_g_ref,
                  out_ref,
                  w_in_ref, w_sq_ref, st_in_ref, st_sq_ref, sem_ref,
                  state_ref, hb_ref, pa_ref, pb_ref, gate_ref, vn_ref, ga_ref, gh_ref, gl_ref, b_ref,
                  qi_ref, ki_ref, qs_ref, ks_ref, a_ref, o_ref, ba_ref, mg_ref):
    @pl.when((pl.program_id(0) == 0) & (pl.program_id(1) == 0))
    def _():
        _stage_weights(w_in_hbm, (w_ba_hbm, w_bb_hbm, w_out_hbm), w_in_ref, w_sq_ref,
                       st_in_ref, st_sq_ref, sem_ref)

    @pl.when(pl.program_id(1) == 0)
    def _():
        state_ref[...] = jnp.zeros_like(state_ref)

    def norm_in():
        hb_ref[...] = _bf(_rms(x_ref[...], norm_g_ref[...]))

    def proj(dst_ref, col0, lo, hi):
        def task():
            dst_ref[:, lo:hi] = _dot(hb_ref[...], w_in_ref[:, col0 + lo:col0 + hi])
        return task

    def act_v(r):
        def task():
            rs = slice(r * CHUNK_B, (r + 1) * CHUNK_B)
            v = _gelu(pa_ref[rs, D_A:2 * D_A])
            mu = jnp.mean(v, axis=-1, keepdims=True)
            vc = v - mu
            inv = lax.rsqrt(jnp.mean(vc * vc, axis=-1, keepdims=True) + LN_EPS)
            vn_ref[rs, :] = _bf(vc * inv * ln_g_ref[...] + ln_b_ref[...])
        return task

    def act_u(r):
        def task():
            rs = slice(r * CHUNK_B, (r + 1) * CHUNK_B)
            ga_ref[rs, :] = _gelu(pa_ref[rs, 0:D_A])
        return task

    def act_z(r):
        def task():
            rs = slice(r * CHUNK_B, (r + 1) * CHUNK_B)
            ga_ref[rs, :] = ga_ref[rs, :] * _silu(pa_ref[rs, 2 * D_A:3 * D_A])
        return task

    def mixa_mix(g):
        def task():
            row = lax.broadcasted_iota(jnp.int32, (CHUNK_A, CHUNK_A), 0)
            col = lax.broadcasted_iota(jnp.int32, (CHUNK_A, CHUNK_A), 1)
            w_g = _bf(jnp.where(row >= col, w_s_ref[g], 0.0))
            cs = slice(g * GROUP_A, (g + 1) * GROUP_A)
            chunks = [slice(c * CHUNK_A, (c + 1) * CHUNK_A) for c in range(N_CA)]
            mixed = _dot(w_g, jnp.concatenate([vn_ref[rs, cs] for rs in chunks], axis=1))
            for c, rs in enumerate(chunks):
                m_c = mixed[:, c * GROUP_A:(c + 1) * GROUP_A] + b_s_ref[:, cs]
                a_ref[rs, cs] = _bf(ga_ref[rs, cs] * m_c)
        return task

    def decay_gate():
        logit = _dot(_bf(pb_ref[:, N_B:N_B + V7X_LANES]), w_gu_ref[...]) + b_gu_ref[...]
        log_alpha = _log_sigmoid(logit) * (1.0 / GATE_NORM)
        g_hi = _bf(log_alpha)
        gh_ref[...] = g_hi
        gl_ref[...] = _bf(log_alpha - g_hi.astype(jnp.float32))

    def decay_cumsum():
        ri = lax.broadcasted_iota(jnp.int32, (TILE_S, TILE_S), 0)
        ci = lax.broadcasted_iota(jnp.int32, (TILE_S, TILE_S), 1)
        tril = _bf(jnp.where((ri >= ci) & (ri // CHUNK_B == ci // CHUNK_B), 1.0, 0.0))
        b_ref[...] = _dot(tril, gh_ref[...]) + _dot(tril, gl_ref[...])

    def gla_scale(c):
        def task():
            rs = slice(c * CHUNK_B, (c + 1) * CHUNK_B)
            b = b_ref[rs, :]
            b_mid = b[CHUNK_B // 2 - 1:CHUNK_B // 2]
            b_last = b[CHUNK_B - 1:CHUNK_B]
            q = pb_ref[rs, 0:KEY_B] * (DK_B ** -0.5)
            k = pb_ref[rs, KEY_B:2 * KEY_B]
            qi_ref[rs, :] = _bf(q * jnp.exp(b - b_mid))
            ki_ref[rs, :] = _bf(k * jnp.exp(b_mid - b))
            qs_ref[rs, :] = _bf(q * jnp.exp(b))
            ks_ref[rs, :] = k * jnp.exp(b_last - b)
        return task

    scores = {}

    def gla_scores(c):
        def task():
            rs = slice(c * CHUNK_B, (c + 1) * CHUNK_B)
            rc = lax.broadcasted_iota(jnp.int32, (CHUNK_B, CHUNK_B), 0)
            cc = lax.broadcasted_iota(jnp.int32, (CHUNK_B, CHUNK_B), 1)
            for h in range(H_B):
                ks = slice(h * DK_B, (h + 1) * DK_B)
                scores[c, h] = _bf(jnp.where(rc >= cc, _dot_nt(qi_ref[rs, ks], ki_ref[rs, ks]), 0.0))
        return task

    def gla_out(c):
        def task():
            rs = slice(c * CHUNK_B, (c + 1) * CHUNK_B)
            for h in range(H_B):
                ks = slice(h * DK_B, (h + 1) * DK_B)
                vs = slice(h * DV_B, (h + 1) * DV_B)
                st = state_ref[h]
                v_h = _bf(pb_ref[rs, 2 * KEY_B + h * DV_B:2 * KEY_B + (h + 1) * DV_B])
                o_h = _dot(scores[c, h], v_h) + _dot(qs_ref[rs, ks], _bf(st))
                decay_row = jnp.exp(b_ref[(c + 1) * CHUNK_B - 1:(c + 1) * CHUNK_B, ks])
                decay_col = jnp.broadcast_to(decay_row, (V7X_SUBLANES, DK_B)).T[:, 0:1]
                state_ref[h] = st * decay_col + _dot(_bf(ks_ref[rs, ks].T), v_h)
                z_b = pb_ref[rs, 2 * KEY_B + VAL_B + h * DV_B:2 * KEY_B + VAL_B + (h + 1) * DV_B]
                o_ref[rs, vs] = _bf(_rms(o_h, gla_g_ref[...]) * _silu(z_b))
        return task

    def gate(lo, hi):
        def task():
            gate_ref[:, lo:hi] = _sigmoid(_dot(hb_ref[...], w_in_ref[:, COL_G + lo:COL_G + hi]))
        return task

    def branch_a(lo, hi):
        def task():
            ba_ref[:, lo:hi] = gate_ref[:, lo:hi] * _dot(a_ref[...], w_sq_ref[:, COL_BA + lo:COL_BA + hi])
        return task

    half = TILE_S // 2

    def branch_b(r):
        def task():
            rs = slice(r * half, (r + 1) * half)
            p_b = _dot(o_ref[rs, :], w_sq_ref[:, COL_BB:COL_BB + D_MODEL])
            mg_ref[rs, :] = _bf(ba_ref[rs, :] + gate_ref[rs, D_MODEL:] * p_b)
        return task

    def out_proj(r):
        def task():
            rs = slice(r * half, (r + 1) * half)
            y = x_ref[rs, :] + _dot(mg_ref[rs, :], w_sq_ref[:, COL_OUT:COL_OUT + D_MODEL])
            out_ref[rs, :] = _rms(y, fin_g_ref[...])
        return task

    pieces = lambda n: [(lo, min(lo + PIECE_N, n)) for lo in range(0, n, PIECE_N)]
    proj_a = [proj(pa_ref, COL_A, lo, hi) for lo, hi in pieces(N_A)]
    proj_qk = proj(pb_ref, COL_B, 0, 2 * KEY_B)
    proj_zl = proj(pb_ref, COL_B, 2 * KEY_B + VAL_B + PIECE_N, N_B + V7X_LANES)
    proj_v = [proj(pb_ref, COL_B, 2 * KEY_B + lo, 2 * KEY_B + hi) for lo, hi in pieces(VAL_B)]
    proj_z = proj(pb_ref, COL_B, 2 * KEY_B + VAL_B, 2 * KEY_B + VAL_B + PIECE_N)
    gates = [gate(lo, hi) for lo, hi in pieces(N_G)]
    mix = [mixa_mix(g) for g in range(H_A)]
    rows = range(N_CB)

    order = [norm_in, proj_a[2], proj_a[3]] + [act_v(r) for r in rows]
    order += [proj_a[0], proj_a[1]] + [act_u(r) for r in rows]
    order += [proj_a[4], proj_a[5]] + [act_z(r) for r in rows]
    order += [proj_qk, proj_zl, decay_gate, proj_v[0], proj_v[1], decay_cumsum]
    order += [gla_scale(c) for c in rows]
    order += [proj_z, mix[0], mix[1]]
    for c in rows:
        order += [gla_scores(c), gates[c], gla_out(c), mix[2 + c]]
    order += [mix[6], mix[7]]
    order += [branch_a(lo, hi) for lo, hi in pieces(D_MODEL)]
    order += [branch_b(0), branch_b(1), out_proj(0), out_proj(1)]
    for task in order:
        task()


def _const_spec(shape):
    nd = len(shape)
    return pl.BlockSpec(shape, lambda b, s: (0,) * nd, pipeline_mode=pl.Buffered(1))


N_STAGE_SLOTS = 2
N_WEIGHT_STREAMS = 4
W_IN_SRC_COLS = N_A + N_B + GATE_RANK + N_G

_WEIGHT_SCRATCH = (
    ((D_MODEL, W_IN_COLS), jnp.bfloat16),
    ((D_MODEL, W_SQ_COLS), jnp.bfloat16),
    ((N_STAGE_SLOTS, STAGE_ROWS, W_IN_SRC_COLS), jnp.float32),
    ((N_STAGE_SLOTS, 3, STAGE_ROWS, D_MODEL), jnp.float32),
)

_SCRATCH = (
    ((H_B, DK_B, DV_B), jnp.float32),
    ((TILE_S, D_MODEL), jnp.bfloat16),
    ((TILE_S, N_A), jnp.float32),
    ((TILE_S, N_B + V7X_LANES), jnp.float32),
    ((TILE_S, N_G), jnp.float32),
    ((TILE_S, D_A), jnp.bfloat16),
    ((TILE_S, D_A), jnp.float32),
    ((TILE_S, KEY_B), jnp.bfloat16),
    ((TILE_S, KEY_B), jnp.bfloat16),
    ((TILE_S, KEY_B), jnp.float32),
    ((TILE_S, KEY_B), jnp.bfloat16),
    ((TILE_S, KEY_B), jnp.bfloat16),
    ((TILE_S, KEY_B), jnp.bfloat16),
    ((TILE_S, KEY_B), jnp.float32),
    ((TILE_S, D_MODEL), jnp.bfloat16),
    ((TILE_S, D_MODEL), jnp.bfloat16),
    ((TILE_S, D_MODEL), jnp.float32),
    ((TILE_S, D_MODEL), jnp.bfloat16),
)


def _vmem_limit_bytes():
    nbytes = lambda shape, dt: math.prod(shape) * jnp.dtype(dt).itemsize
    params = 2 * V7X_LANES * KEY_B + 4 * (H_A * CHUNK_A * CHUNK_A + CHUNK_A * D_A)
    io_windows = 2 * 2 * TILE_S * D_MODEL * 4
    scratch = sum(nbytes(s, d) for s, d in _WEIGHT_SCRATCH + _SCRATCH)
    temporaries = 12 * TILE_S * D_MODEL * 4
    return min(params + io_windows + scratch + temporaries, V7X_VMEM_BYTES - 8 * 1024 * 1024)


@jax.jit
def _forward(x, norm_g, w_in, ln_v_g, ln_v_b, w_spatial, b_spatial, w_gate_up, b_gate_up,
             gla_norm_g, w_branch_a, w_branch_b, w_out, final_norm_g):
    bsz, seq, d = x.shape
    assert d == D_MODEL and seq % TILE_S == 0
    assert TILE_S % CHUNK_A == 0 and TILE_S % CHUNK_B == 0
    assert w_in.shape == (D_MODEL, W_IN_SRC_COLS)
    w_gu = jnp.pad(w_gate_up, ((0, V7X_LANES - GATE_RANK), (0, 0))).astype(jnp.bfloat16)
    b_s = jnp.repeat(b_spatial.T, GROUP_A, axis=1)
    row = lambda a: a.reshape(1, -1)

    hbm_weights = (w_in, w_branch_a, w_branch_b, w_out)
    params = (row(ln_v_g), row(ln_v_b), w_spatial, b_s, w_gu, row(b_gate_up), row(gla_norm_g),
              row(final_norm_g))
    operands = (x, row(norm_g)) + hbm_weights + params
    in_specs = [pl.BlockSpec((None, TILE_S, D_MODEL), lambda b, s: (b, s, 0)), _const_spec((1, D_MODEL))]
    in_specs += [pl.BlockSpec(memory_space=pl.ANY) for _ in hbm_weights]
    in_specs += [_const_spec(op.shape) for op in params]
    return pl.pallas_call(
        _layer_kernel,
        grid=(bsz, seq // TILE_S),
        in_specs=in_specs,
        out_specs=pl.BlockSpec((None, TILE_S, D_MODEL), lambda b, s: (b, s, 0)),
        out_shape=jax.ShapeDtypeStruct(x.shape, x.dtype),
        scratch_shapes=[pltpu.VMEM(s, d) for s, d in _WEIGHT_SCRATCH]
        + [pltpu.SemaphoreType.DMA((N_STAGE_SLOTS, N_WEIGHT_STREAMS))]
        + [pltpu.VMEM(s, d) for s, d in _SCRATCH],
        compiler_params=pltpu.CompilerParams(
            dimension_semantics=("arbitrary", "arbitrary"),
            vmem_limit_bytes=_vmem_limit_bytes(),
        ),
        name="hybrid_layer",
    )(*operands)


def kernel(x, norm_g, w_in, ln_v_g, ln_v_b, w_spatial, b_spatial, w_gate_up, b_gate_up, gla_norm_g,
           w_branch_a, w_branch_b, w_out, final_norm_g):
    assert norm_g.shape[0] == 1, "single-layer problem"
    return _forward(x, norm_g[0], w_in[0], ln_v_g[0], ln_v_b[0], w_spatial[0], b_spatial[0],
                    w_gate_up[0], b_gate_up[0], gla_norm_g[0], w_branch_a[0], w_branch_b[0],
                    w_out[0], final_norm_g)
```

```python
import math

import jax
import jax.numpy as jnp
from jax import lax
from jax.experimental import pallas as pl
from jax.experimental.pallas import tpu as pltpu

D_MODEL = 1024
D_A = D_MODEL
H_A = 8
GROUP_A = D_A // H_A
CHUNK_A = 128
H_B = 4
KEY_B = D_MODEL // 2
VAL_B = D_MODEL
DK_B = KEY_B // H_B
DV_B = VAL_B // H_B
GATE_RANK = 16
GATE_NORM = 16.0
CHUNK_B = 64
EPS = 1e-6
LN_EPS = 1e-5

V7X_LANES = 128
V7X_SUBLANES = 8
V7X_VMEM_BYTES = 64 * 1024 * 1024
TILE_S = 256
PIECE_N = 512
STAGE_ROWS = 64
STAGE_COLS_IN = 512

N_A = 3 * D_A
N_B = 2 * KEY_B + 2 * VAL_B
N_G = 2 * D_MODEL
N_CA = TILE_S // CHUNK_A
N_CB = TILE_S // CHUNK_B
SRC_G = N_A + N_B + GATE_RANK

COL_A = 0
COL_B = COL_A + N_A
COL_LR = COL_B + N_B
COL_G = COL_LR + V7X_LANES
W_IN_COLS = COL_G + N_G
COL_BA = 0
COL_BB = COL_BA + D_MODEL
COL_OUT = COL_BB + D_MODEL
W_SQ_COLS = COL_OUT + D_MODEL + V7X_LANES
assert (W_IN_COLS // V7X_LANES) % 2 == 1 and (W_SQ_COLS // V7X_LANES) % 2 == 1
assert COL_LR % STAGE_COLS_IN == 0 and N_G % STAGE_COLS_IN == 0


def _bf(x):
    return x.astype(jnp.bfloat16)


def _dot(a, b):
    return jnp.dot(a, b, preferred_element_type=jnp.float32)


def _dot_nt(a, b):
    return lax.dot_general(a, b, (((1,), (1,)), ((), ())), preferred_element_type=jnp.float32)


def _gelu(x):
    return 0.5 * x * (1.0 + lax.erf(x * (1.0 / math.sqrt(2.0))))


def _sigmoid(x):
    return 1.0 / (1.0 + jnp.exp(-x))


def _silu(x):
    return x * _sigmoid(x)


def _log_sigmoid(x):
    return jnp.minimum(x, 0.0) - jnp.log1p(jnp.exp(-jnp.abs(x)))


def _rms(x, g):
    return x * lax.rsqrt(jnp.mean(x * x, axis=-1, keepdims=True) + EPS) * g


def _stage_chunks(n_chunks, copies, store):
    for cp in copies(0, 0):
        cp.start()

    def body(i, carry):
        slot = lax.rem(i, N_STAGE_SLOTS)

        @pl.when(i + 1 < n_chunks)
        def _():
            for cp in copies(i + 1, lax.rem(i + 1, N_STAGE_SLOTS)):
                cp.start()

        for cp in copies(i, slot):
            cp.wait()
        store(i, slot)
        return carry

    lax.fori_loop(0, n_chunks, body, 0)


def _stage_weights(w_in_t_hbm, w_sq_hbm, w_in_ref, w_sq_ref, st_in_ref, st_sq_ref, sem_ref):
    def in_chunks(src0, dst0):
        def copies(i, slot):
            rows = pl.ds(pl.multiple_of(src0 + i * STAGE_COLS_IN, 16), STAGE_COLS_IN)
            return [pltpu.make_async_copy(w_in_t_hbm.at[rows, :], st_in_ref.at[slot], sem_ref.at[slot, 0])]

        def store(i, slot):
            cols = pl.ds(pl.multiple_of(dst0 + i * STAGE_COLS_IN, STAGE_COLS_IN), STAGE_COLS_IN)
            w_in_ref[:, cols] = _bf(st_in_ref[slot].T)
        return copies, store

    def sq_copies(i, slot):
        rows = pl.ds(pl.multiple_of(i * STAGE_ROWS, STAGE_ROWS), STAGE_ROWS)
        return [pltpu.make_async_copy(w.at[rows, :], st_sq_ref.at[slot, k], sem_ref.at[slot, 1 + k])
                for k, w in enumerate(w_sq_hbm)]

    def sq_store(i, slot):
        rows = pl.ds(pl.multiple_of(i * STAGE_ROWS, STAGE_ROWS), STAGE_ROWS)
        for k in range(len(w_sq_hbm)):
            w_sq_ref[rows, k * D_MODEL:(k + 1) * D_MODEL] = _bf(st_sq_ref[slot, k])

    _stage_chunks(COL_LR // STAGE_COLS_IN, *in_chunks(0, 0))
    _stage_chunks(N_G // STAGE_COLS_IN, *in_chunks(SRC_G, COL_G))
    _stage_chunks(D_MODEL // STAGE_ROWS, sq_copies, sq_store)
    lr_copy = pltpu.make_async_copy(w_in_t_hbm.at[pl.ds(COL_LR, GATE_RANK), :],
                                    st_in_ref.at[0, pl.ds(0, GATE_RANK), :], sem_ref.at[0, 0])
    lr_copy.start()
    lr_copy.wait()
    w_in_ref[:, COL_LR:COL_G] = jnp.zeros((D_MODEL, V7X_LANES), jnp.bfloat16)
    w_in_ref[:, COL_LR:COL_LR + GATE_RANK] = _bf(st_in_ref[0, 0:GATE_RANK, :].T)


def _layer_kernel(x_ref, norm_g_ref, w_in_hbm, w_ba_hbm, w_bb_hbm, w_out_hbm,
                  ln_g_ref, ln_b_ref, w_s_ref, b_s_ref, w_gu_ref, b_gu_ref, gla_g_ref, fin_g_ref,
                  out_ref,
                  w_in_ref, w_sq_ref, st_in_ref, st_sq_ref, sem_ref,
                  state_ref, hb_ref, pa_ref, pb_ref, gate_ref, vn_ref, ga_ref, gh_ref, gl_ref, b_ref,
                  qi_ref, ki_ref, qs_ref, ks_ref, a_ref, o_ref, ba_ref, mg_ref):
    @pl.when((pl.program_id(0) == 0) & (pl.program_id(1) == 0))
    def _():
        _stage_weights(w_in_hbm, (w_ba_hbm, w_bb_hbm, w_out_hbm), w_in_ref, w_sq_ref,
                       st_in_ref, st_sq_ref, sem_ref)

    @pl.when(pl.program_id(1) == 0)
    def _():
        state_ref[...] = jnp.zeros_like(state_ref)

    def norm_in():
        hb_ref[...] = _bf(_rms(x_ref[...], norm_g_ref[...]))

    def proj(dst_ref, col0, lo, hi):
        def task():
            dst_ref[:, lo:hi] = _dot(hb_ref[...], w_in_ref[:, col0 + lo:col0 + hi])
        return task

    def act_v(r):
        def task():
            rs = slice(r * CHUNK_B, (r + 1) * CHUNK_B)
            v = _gelu(pa_ref[rs, D_A:2 * D_A])
            mu = jnp.mean(v, axis=-1, keepdims=True)
            vc = v - mu
            inv = lax.rsqrt(jnp.mean(vc * vc, axis=-1, keepdims=True) + LN_EPS)
            vn_ref[rs, :] = _bf(vc * inv * ln_g_ref[...] + ln_b_ref[...])
        return task

    def act_u(r):
        def task():
            rs = slice(r * CHUNK_B, (r + 1) * CHUNK_B)
            ga_ref[rs, :] = _gelu(pa_ref[rs, 0:D_A])
        return task

    def act_z(r):
        def task():
            rs = slice(r * CHUNK_B, (r + 1) * CHUNK_B)
            ga_ref[rs, :] = ga_ref[rs, :] * _silu(pa_ref[rs, 2 * D_A:3 * D_A])
        return task

    def mixa_mix(g):
        def task():
            row = lax.broadcasted_iota(jnp.int32, (CHUNK_A, CHUNK_A), 0)
            col = lax.broadcasted_iota(jnp.int32, (CHUNK_A, CHUNK_A), 1)
            w_g = _bf(jnp.where(row >= col, w_s_ref[g], 0.0))
            cs = slice(g * GROUP_A, (g + 1) * GROUP_A)
            chunks = [slice(c * CHUNK_A, (c + 1) * CHUNK_A) for c in range(N_CA)]
            mixed = _dot(w_g, jnp.concatenate([vn_ref[rs, cs] for rs in chunks], axis=1))
            for c, rs in enumerate(chunks):
                m_c = mixed[:, c * GROUP_A:(c + 1) * GROUP_A] + b_s_ref[:, cs]
                a_ref[rs, cs] = _bf(ga_ref[rs, cs] * m_c)
        return task

    def decay_gate():
        logit = _dot(_bf(pb_ref[:, N_B:N_B + V7X_LANES]), w_gu_ref[...]) + b_gu_ref[...]
        log_alpha = _log_sigmoid(logit) * (1.0 / GATE_NORM)
        g_hi = _bf(log_alpha)
        gh_ref[...] = g_hi
        gl_ref[...] = _bf(log_alpha - g_hi.astype(jnp.float32))

    def decay_cumsum():
        ri = lax.broadcasted_iota(jnp.int32, (TILE_S, TILE_S), 0)
        ci = lax.broadcasted_iota(jnp.int32, (TILE_S, TILE_S), 1)
        tril = _bf(jnp.where((ri >= ci) & (ri // CHUNK_B == ci // CHUNK_B), 1.0, 0.0))
        b_ref[...] = _dot(tril, gh_ref[...]) + _dot(tril, gl_ref[...])

    def gla_scale(c):
        def task():
            rs = slice(c * CHUNK_B, (c + 1) * CHUNK_B)
            b = b_ref[rs, :]
            b_mid = b[CHUNK_B // 2 - 1:CHUNK_B // 2]
            b_last = b[CHUNK_B - 1:CHUNK_B]
            q = pb_ref[rs, 0:KEY_B] * (DK_B ** -0.5)
            k = pb_ref[rs, KEY_B:2 * KEY_B]
            qi_ref[rs, :] = _bf(q * jnp.exp(b - b_mid))
            ki_ref[rs, :] = _bf(k * jnp.exp(b_mid - b))
            qs_ref[rs, :] = _bf(q * jnp.exp(b))
            ks_ref[rs, :] = k * jnp.exp(b_last - b)
        return task

    scores = {}

    def gla_scores(c):
        def task():
            rs = slice(c * CHUNK_B, (c + 1) * CHUNK_B)
            rc = lax.broadcasted_iota(jnp.int32, (CHUNK_B, CHUNK_B), 0)
            cc = lax.broadcasted_iota(jnp.int32, (CHUNK_B, CHUNK_B), 1)
            for h in range(H_B):
                ks = slice(h * DK_B, (h + 1) * DK_B)
                scores[c, h] = _bf(jnp.where(rc >= cc, _dot_nt(qi_ref[rs, ks], ki_ref[rs, ks]), 0.0))
        return task

    def gla_out(c):
        def task():
            rs = slice(c * CHUNK_B, (c + 1) * CHUNK_B)
            for h in range(H_B):
                ks = slice(h * DK_B, (h + 1) * DK_B)
                vs = slice(h * DV_B, (h + 1) * DV_B)
                st = state_ref[h]
                v_h = _bf(pb_ref[rs, 2 * KEY_B + h * DV_B:2 * KEY_B + (h + 1) * DV_B])
                o_h = _dot(scores[c, h], v_h) + _dot(qs_ref[rs, ks], _bf(st))
                decay_row = jnp.exp(b_ref[(c + 1) * CHUNK_B - 1:(c + 1) * CHUNK_B, ks])
                decay_col = jnp.broadcast_to(decay_row, (V7X_SUBLANES, DK_B)).T[:, 0:1]
                state_ref[h] = st * decay_col + _dot(_bf(ks_ref[rs, ks].T), v_h)
                z_b = pb_ref[rs, 2 * KEY_B + VAL_B + h * DV_B:2 * KEY_B + VAL_B + (h + 1) * DV_B]
                o_ref[rs, vs] = _bf(_rms(o_h, gla_g_ref[...]) * _silu(z_b))
        return task

    def gate(lo, hi):
        def task():
            gate_ref[:, lo:hi] = _sigmoid(_dot(hb_ref[...], w_in_ref[:, COL_G + lo:COL_G + hi]))
        return task

    def branch_a(lo, hi):
        def task():
            ba_ref[:, lo:hi] = gate_ref[:, lo:hi] * _dot(a_ref[...], w_sq_ref[:, COL_BA + lo:COL_BA + hi])
        return task

    half = TILE_S // 2

    def branch_b(r):
        def task():
            rs = slice(r * half, (r + 1) * half)
            p_b = _dot(o_ref[rs, :], w_sq_ref[:, COL_BB:COL_BB + D_MODEL])
            mg_ref[rs, :] = _bf(ba_ref[rs, :] + gate_ref[rs, D_MODEL:] * p_b)
        return task

    def out_proj(r):
        def task():
            rs = slice(r * half, (r + 1) * half)
            y = x_ref[rs, :] + _dot(mg_ref[rs, :], w_sq_ref[:, COL_OUT:COL_OUT + D_MODEL])
            out_ref[rs, :] = _rms(y, fin_g_ref[...])
        return task

    pieces = lambda n: [(lo, min(lo + PIECE_N, n)) for lo in range(0, n, PIECE_N)]
    proj_a = [proj(pa_ref, COL_A, lo, hi) for lo, hi in pieces(N_A)]
    proj_qk = proj(pb_ref, COL_B, 0, 2 * KEY_B)
    proj_zl = proj(pb_ref, COL_B, 2 * KEY_B + VAL_B + PIECE_N, N_B + V7X_LANES)
    proj_v = [proj(pb_ref, COL_B, 2 * KEY_B + lo, 2 * KEY_B + hi) for lo, hi in pieces(VAL_B)]
    proj_z = proj(pb_ref, COL_B, 2 * KEY_B + VAL_B, 2 * KEY_B + VAL_B + PIECE_N)
    gates = [gate(lo, hi) for lo, hi in pieces(N_G)]
    mix = [mixa_mix(g) for g in range(H_A)]
    rows = range(N_CB)

    order = [norm_in, proj_a[2], proj_a[3]] + [act_v(r) for r in rows]
    order += [proj_a[0], proj_a[1]] + [act_u(r) for r in rows]
    order += [proj_a[4], proj_a[5]] + [act_z(r) for r in rows]
    order += [proj_qk, proj_zl, decay_gate, proj_v[0], proj_v[1], decay_cumsum]
    order += [gla_scale(c) for c in rows]
    order += [proj_z]
    n_pairs = N_CB // 2
    gates_per, mix_per = len(gates) // n_pairs, len(mix) // N_CB
    for j in range(n_pairs):
        order += [gla_scores(2 * j), gla_scores(2 * j + 1)] + gates[j * gates_per:(j + 1) * gates_per]
        for c in (2 * j, 2 * j + 1):
            order += [gla_out(c)] + mix[c * mix_per:(c + 1) * mix_per]
    order += [branch_a(lo, hi) for lo, hi in pieces(D_MODEL)]
    order += [branch_b(0), branch_b(1), out_proj(0), out_proj(1)]
    for task in order:
        task()


def _const_spec(shape):
    nd = len(shape)
    return pl.BlockSpec(shape, lambda b, s: (0,) * nd, pipeline_mode=pl.Buffered(1))


N_STAGE_SLOTS = 2
N_WEIGHT_STREAMS = 4

_WEIGHT_SCRATCH = (
    ((D_MODEL, W_IN_COLS), jnp.bfloat16),
    ((D_MODEL, W_SQ_COLS), jnp.bfloat16),
    ((N_STAGE_SLOTS, STAGE_COLS_IN, D_MODEL), jnp.float32),
    ((N_STAGE_SLOTS, 3, STAGE_ROWS, D_MODEL), jnp.float32),
)

_SCRATCH = (
    ((H_B, DK_B, DV_B), jnp.float32),
    ((TILE_S, D_MODEL), jnp.bfloat16),
    ((TILE_S, N_A), jnp.float32),
    ((TILE_S, N_B + V7X_LANES), jnp.float32),
    ((TILE_S, N_G), jnp.float32),
    ((TILE_S, D_A), jnp.bfloat16),
    ((TILE_S, D_A), jnp.float32),
    ((TILE_S, KEY_B), jnp.bfloat16),
    ((TILE_S, KEY_B), jnp.bfloat16),
    ((TILE_S, KEY_B), jnp.float32),
    ((TILE_S, KEY_B), jnp.bfloat16),
    ((TILE_S, KEY_B), jnp.bfloat16),
    ((TILE_S, KEY_B), jnp.bfloat16),
    ((TILE_S, KEY_B), jnp.float32),
    ((TILE_S, D_MODEL), jnp.bfloat16),
    ((TILE_S, D_MODEL), jnp.bfloat16),
    ((TILE_S, D_MODEL), jnp.float32),
    ((TILE_S, D_MODEL), jnp.bfloat16),
)


def _vmem_limit_bytes():
    nbytes = lambda shape, dt: math.prod(shape) * jnp.dtype(dt).itemsize
    params = 2 * V7X_LANES * KEY_B + 4 * (H_A * CHUNK_A * CHUNK_A + CHUNK_A * D_A)
    io_windows = 2 * 2 * TILE_S * D_MODEL * 4
    scratch = sum(nbytes(s, d) for s, d in _WEIGHT_SCRATCH + _SCRATCH)
    temporaries = 12 * TILE_S * D_MODEL * 4
    return min(params + io_windows + scratch + temporaries, V7X_VMEM_BYTES - 8 * 1024 * 1024)


@jax.jit
def _forward(x, norm_g, w_in, ln_v_g, ln_v_b, w_spatial, b_spatial, w_gate_up, b_gate_up,
             gla_norm_g, w_branch_a, w_branch_b, w_out, final_norm_g):
    bsz, seq, d = x.shape
    assert d == D_MODEL and seq % TILE_S == 0
    assert TILE_S % CHUNK_A == 0 and TILE_S % CHUNK_B == 0
    assert w_in.shape == (D_MODEL, SRC_G + N_G)
    w_gu = jnp.pad(w_gate_up, ((0, V7X_LANES - GATE_RANK), (0, 0))).astype(jnp.bfloat16)
    b_s = jnp.repeat(b_spatial.T, GROUP_A, axis=1)
    row = lambda a: a.reshape(1, -1)

    hbm_weights = (w_in.T, w_branch_a, w_branch_b, w_out)
    params = (row(ln_v_g), row(ln_v_b), w_spatial, b_s, w_gu, row(b_gate_up), row(gla_norm_g),
              row(final_norm_g))
    operands = (x, row(norm_g)) + hbm_weights + params
    in_specs = [pl.BlockSpec((None, TILE_S, D_MODEL), lambda b, s: (b, s, 0)), _const_spec((1, D_MODEL))]
    in_specs += [pl.BlockSpec(memory_space=pl.ANY) for _ in hbm_weights]
    in_specs += [_const_spec(op.shape) for op in params]
    return pl.pallas_call(
        _layer_kernel,
        grid=(bsz, seq // TILE_S),
        in_specs=in_specs,
        out_specs=pl.BlockSpec((None, TILE_S, D_MODEL), lambda b, s: (b, s, 0)),
        out_shape=jax.ShapeDtypeStruct(x.shape, x.dtype),
        scratch_shapes=[pltpu.VMEM(s, d) for s, d in _WEIGHT_SCRATCH]
        + [pltpu.SemaphoreType.DMA((N_STAGE_SLOTS, N_WEIGHT_STREAMS))]
        + [pltpu.VMEM(s, d) for s, d in _SCRATCH],
        compiler_params=pltpu.CompilerParams(
            dimension_semantics=("arbitrary", "arbitrary"),
            vmem_limit_bytes=_vmem_limit_bytes(),
        ),
        name="hybrid_layer",
    )(*operands)


def kernel(x, norm_g, w_in, ln_v_g, ln_v_b, w_spatial, b_spatial, w_gate_up, b_gate_up, gla_norm_g,
           w_branch_a, w_branch_b, w_out, final_norm_g):
    assert norm_g.shape[0] == 1, "single-layer problem"
    return _forward(x, norm_g[0], w_in[0], ln_v_g[0], ln_v_b[0], w_spatial[0], b_spatial[0],
                    w_gate_up[0], b_gate_up[0], gla_norm_g[0], w_branch_a[0], w_branch_b[0],
                    w_out[0], final_norm_g)
```

```python
import functools
import math

import jax
import jax.numpy as jnp
from jax import lax
from jax.experimental import pallas as pl
from jax.experimental.pallas import tpu as pltpu

D_MODEL = 1024
D_A = D_MODEL
H_A = 8
GROUP_A = D_A // H_A
CHUNK_A = 128
H_B = 4
KEY_B = D_MODEL // 2
VAL_B = D_MODEL
DK_B = KEY_B // H_B
DV_B = VAL_B // H_B
GATE_RANK = 16
GATE_NORM = 16.0
CHUNK_B = 64
EPS = 1e-6
LN_EPS = 1e-5

V7X_LANES = 128
V7X_SUBLANES = 8
V7X_VMEM_BYTES = 64 * 1024 * 1024
TILE_S = 256
PIECE_N = 512
STAGE_ROWS = 64
STAGE_COLS_IN = 512

N_A = 3 * D_A
N_B = 2 * KEY_B + 2 * VAL_B
N_G = 2 * D_MODEL
N_CA = TILE_S // CHUNK_A
N_CB = TILE_S // CHUNK_B
SRC_G = N_A + N_B + GATE_RANK

COL_A = 0
COL_B = COL_A + N_A
COL_LR = COL_B + N_B
COL_G = COL_LR + V7X_LANES
W_IN_COLS = COL_G + N_G
COL_BA = 0
COL_BB = COL_BA + D_MODEL
COL_OUT = COL_BB + D_MODEL
W_SQ_COLS = COL_OUT + D_MODEL + V7X_LANES
assert (W_IN_COLS // V7X_LANES) % 2 == 1 and (W_SQ_COLS // V7X_LANES) % 2 == 1
assert COL_LR % STAGE_COLS_IN == 0 and N_G % STAGE_COLS_IN == 0


def _bf(x):
    return x.astype(jnp.bfloat16)


def _dot(a, b):
    return jnp.dot(a, b, preferred_element_type=jnp.float32)


def _dot_nt(a, b):
    return lax.dot_general(a, b, (((1,), (1,)), ((), ())), preferred_element_type=jnp.float32)


def _gelu(x):
    return 0.5 * x * (1.0 + lax.erf(x * (1.0 / math.sqrt(2.0))))


def _sigmoid(x):
    return 1.0 / (1.0 + jnp.exp(-x))


def _silu(x):
    return x * _sigmoid(x)


def _log_sigmoid(x):
    return jnp.minimum(x, 0.0) - jnp.log(1.0 + jnp.exp(-jnp.abs(x)))


def _rms(x, g):
    return x * lax.rsqrt(jnp.mean(x * x, axis=-1, keepdims=True) + EPS) * g


def _stage_chunks(n_chunks, copies, store):
    for cp in copies(0, 0):
        cp.start()

    def body(i, carry):
        slot = lax.rem(i, N_STAGE_SLOTS)

        @pl.when(i + 1 < n_chunks)
        def _():
            for cp in copies(i + 1, lax.rem(i + 1, N_STAGE_SLOTS)):
                cp.start()

        for cp in copies(i, slot):
            cp.wait()
        store(i, slot)
        return carry

    lax.fori_loop(0, n_chunks, body, 0)


def _stage_weights(w_in_t_hbm, w_sq_hbm, w_in_ref, w_sq_ref, st_in_ref, st_sq_ref, sem_ref):
    def in_chunks(src0, dst0):
        def copies(i, slot):
            rows = pl.ds(pl.multiple_of(src0 + i * STAGE_COLS_IN, 16), STAGE_COLS_IN)
            return [pltpu.make_async_copy(w_in_t_hbm.at[rows, :], st_in_ref.at[slot], sem_ref.at[slot, 0])]

        def store(i, slot):
            cols = pl.ds(pl.multiple_of(dst0 + i * STAGE_COLS_IN, STAGE_COLS_IN), STAGE_COLS_IN)
            w_in_ref[:, cols] = _bf(st_in_ref[slot].T)
        return copies, store

    def sq_copies(i, slot):
        rows = pl.ds(pl.multiple_of(i * STAGE_ROWS, STAGE_ROWS), STAGE_ROWS)
        return [pltpu.make_async_copy(w.at[rows, :], st_sq_ref.at[slot, k], sem_ref.at[slot, 1 + k])
                for k, w in enumerate(w_sq_hbm)]

    def sq_store(i, slot):
        rows = pl.ds(pl.multiple_of(i * STAGE_ROWS, STAGE_ROWS), STAGE_ROWS)
        for k in range(len(w_sq_hbm)):
            w_sq_ref[rows, k * D_MODEL:(k + 1) * D_MODEL] = _bf(st_sq_ref[slot, k])

    _stage_chunks(COL_LR // STAGE_COLS_IN, *in_chunks(0, 0))
    _stage_chunks(N_G // STAGE_COLS_IN, *in_chunks(SRC_G, COL_G))
    _stage_chunks(D_MODEL // STAGE_ROWS, sq_copies, sq_store)
    lr_copy = pltpu.make_async_copy(w_in_t_hbm.at[pl.ds(COL_LR, GATE_RANK), :],
                                    st_in_ref.at[0, pl.ds(0, GATE_RANK), :], sem_ref.at[0, 0])
    lr_copy.start()
    lr_copy.wait()
    w_in_ref[:, COL_LR:COL_G] = jnp.zeros((D_MODEL, V7X_LANES), jnp.bfloat16)
    w_in_ref[:, COL_LR:COL_LR + GATE_RANK] = _bf(st_in_ref[0, 0:GATE_RANK, :].T)


def _layer_kernel(tiles_per_row,
                  x_ref, xn_ref, norm_g_ref, w_in_hbm, w_ba_hbm, w_bb_hbm, w_out_hbm,
                  ln_g_ref, ln_b_ref, w_s_ref, b_s_ref, w_gu_ref, b_gu_ref, gla_g_ref, fin_g_ref,
                  out_ref,
                  w_in_ref, w_sq_ref, st_in_ref, st_sq_ref, sem_ref,
                  state_ref, hb_ref, y_ref, pa_ref, pb_ref, gate_ref, vn_ref, ga_ref, gh_ref, gl_ref, b_ref,
                  qi_ref, ki_ref, qs_ref, ks_ref, a_ref, o_ref, ba_ref, mg_ref):
    step = pl.program_id(0)

    @pl.when(step == 0)
    def _():
        _stage_weights(w_in_hbm, (w_ba_hbm, w_bb_hbm, w_out_hbm), w_in_ref, w_sq_ref,
                       st_in_ref, st_sq_ref, sem_ref)
        hb_ref[...] = _bf(_rms(x_ref[...], norm_g_ref[...]))
        y_ref[...] = jnp.zeros_like(y_ref)

    @pl.when(lax.rem(step, tiles_per_row) == 0)
    def _():
        state_ref[...] = jnp.zeros_like(state_ref)

    def norm_prev(r):
        def task():
            rs = slice(r * CHUNK_B, (r + 1) * CHUNK_B)
            out_ref[rs, :] = _rms(y_ref[rs, :], fin_g_ref[...])
        return task

    def norm_next(r):
        def task():
            rs = slice(r * CHUNK_B, (r + 1) * CHUNK_B)
            hb_ref[rs, :] = _bf(_rms(xn_ref[rs, :], norm_g_ref[...]))
        return task

    def proj(dst_ref, col0, lo, hi):
        def task():
            dst_ref[:, lo:hi] = _dot(hb_ref[...], w_in_ref[:, col0 + lo:col0 + hi])
        return task

    def act_v(r):
        def task():
            rs = slice(r * CHUNK_B, (r + 1) * CHUNK_B)
            v = _gelu(pa_ref[rs, D_A:2 * D_A])
            mu = jnp.mean(v, axis=-1, keepdims=True)
            vc = v - mu
            inv = lax.rsqrt(jnp.mean(vc * vc, axis=-1, keepdims=True) + LN_EPS)
            vn_ref[rs, :] = _bf(vc * inv * ln_g_ref[...] + ln_b_ref[...])
        return task

    def act_u(r):
        def task():
            rs = slice(r * CHUNK_B, (r + 1) * CHUNK_B)
            ga_ref[rs, :] = _gelu(pa_ref[rs, 0:D_A])
        return task

    def act_z(r):
        def task():
            rs = slice(r * CHUNK_B, (r + 1) * CHUNK_B)
            ga_ref[rs, :] = ga_ref[rs, :] * _silu(pa_ref[rs, 2 * D_A:3 * D_A])
        return task

    def mixa_mix(g):
        def task():
            row = lax.broadcasted_iota(jnp.int32, (CHUNK_A, CHUNK_A), 0)
            col = lax.broadcasted_iota(jnp.int32, (CHUNK_A, CHUNK_A), 1)
            w_g = _bf(jnp.where(row >= col, w_s_ref[g], 0.0))
            cs = slice(g * GROUP_A, (g + 1) * GROUP_A)
            chunks = [slice(c * CHUNK_A, (c + 1) * CHUNK_A) for c in range(N_CA)]
            mixed = _dot(w_g, jnp.concatenate([vn_ref[rs, cs] for rs in chunks], axis=1))
            for c, rs in enumerate(chunks):
                m_c = mixed[:, c * GROUP_A:(c + 1) * GROUP_A] + b_s_ref[:, cs]
                a_ref[rs, cs] = _bf(ga_ref[rs, cs] * m_c)
        return task

    def decay_gate():
        logit = _dot(_bf(pb_ref[:, N_B:N_B + V7X_LANES]), w_gu_ref[...]) + b_gu_ref[...]
        log_alpha = _log_sigmoid(logit) * (1.0 / GATE_NORM)
        g_hi = _bf(log_alpha)
        gh_ref[...] = g_hi
        gl_ref[...] = _bf(log_alpha - g_hi.astype(jnp.float32))

    def decay_cumsum():
        ri = lax.broadcasted_iota(jnp.int32, (TILE_S, TILE_S), 0)
        ci = lax.broadcasted_iota(jnp.int32, (TILE_S, TILE_S), 1)
        tril = _bf(jnp.where((ri >= ci) & (ri // CHUNK_B == ci // CHUNK_B), 1.0, 0.0))
        b_ref[...] = _dot(tril, gh_ref[...]) + _dot(tril, gl_ref[...])

    def gla_scale(c):
        def task():
            rs = slice(c * CHUNK_B, (c + 1) * CHUNK_B)
            b = b_ref[rs, :]
            b_mid = b[CHUNK_B // 2 - 1:CHUNK_B // 2]
            b_last = b[CHUNK_B - 1:CHUNK_B]
            q = pb_ref[rs, 0:KEY_B] * (DK_B ** -0.5)
            k = pb_ref[rs, KEY_B:2 * KEY_B]
            qi_ref[rs, :] = _bf(q * jnp.exp(b - b_mid))
            ki_ref[rs, :] = _bf(k * jnp.exp(b_mid - b))
            qs_ref[rs, :] = _bf(q * jnp.exp(b))
            ks_ref[rs, :] = k * jnp.exp(b_last - b)
        return task

    scores = {}

    def gla_scores(c):
        def task():
            rs = slice(c * CHUNK_B, (c + 1) * CHUNK_B)
            rc = lax.broadcasted_iota(jnp.int32, (CHUNK_B, CHUNK_B), 0)
            cc = lax.broadcasted_iota(jnp.int32, (CHUNK_B, CHUNK_B), 1)
            for h in range(H_B):
                ks = slice(h * DK_B, (h + 1) * DK_B)
                scores[c, h] = _bf(jnp.where(rc >= cc, _dot_nt(qi_ref[rs, ks], ki_ref[rs, ks]), 0.0))
        return task

    def gla_out(c):
        def task():
            rs = slice(c * CHUNK_B, (c + 1) * CHUNK_B)
            for h in range(H_B):
                ks = slice(h * DK_B, (h + 1) * DK_B)
                vs = slice(h * DV_B, (h + 1) * DV_B)
                st = state_ref[h]
                v_h = _bf(pb_ref[rs, 2 * KEY_B + h * DV_B:2 * KEY_B + (h + 1) * DV_B])
                o_h = _dot(scores[c, h], v_h) + _dot(qs_ref[rs, ks], _bf(st))
                decay_row = jnp.exp(b_ref[(c + 1) * CHUNK_B - 1:(c + 1) * CHUNK_B, ks])
                decay_col = jnp.broadcast_to(decay_row, (V7X_SUBLANES, DK_B)).T[:, 0:1]
                state_ref[h] = st * decay_col + _dot(_bf(ks_ref[rs, ks].T), v_h)
                z_b = pb_ref[rs, 2 * KEY_B + VAL_B + h * DV_B:2 * KEY_B + VAL_B + (h + 1) * DV_B]
                o_ref[rs, vs] = _bf(_rms(o_h, gla_g_ref[...]) * _silu(z_b))
        return task

    def gate(lo, hi):
        def task():
            gate_ref[:, lo:hi] = _sigmoid(_dot(hb_ref[...], w_in_ref[:, COL_G + lo:COL_G + hi]))
        return task

    def branch_a(lo, hi):
        def task():
            ba_ref[:, lo:hi] = gate_ref[:, lo:hi] * _dot(a_ref[...], w_sq_ref[:, COL_BA + lo:COL_BA + hi])
        return task

    half = TILE_S // 2

    def branch_b(r):
        def task():
            rs = slice(r * half, (r + 1) * half)
            p_b = _dot(o_ref[rs, :], w_sq_ref[:, COL_BB:COL_BB + D_MODEL])
            mg_ref[rs, :] = _bf(ba_ref[rs, :] + gate_ref[rs, D_MODEL:] * p_b)
        return task

    def out_proj(r):
        def task():
            rs = slice(r * half, (r + 1) * half)
            y_ref[rs, :] = x_ref[rs, :] + _dot(mg_ref[rs, :], w_sq_ref[:, COL_OUT:COL_OUT + D_MODEL])
        return task

    pieces = lambda n: [(lo, min(lo + PIECE_N, n)) for lo in range(0, n, PIECE_N)]
    proj_a = [proj(pa_ref, COL_A, lo, hi) for lo, hi in pieces(N_A)]
    proj_qk = proj(pb_ref, COL_B, 0, 2 * KEY_B)
    proj_zl = proj(pb_ref, COL_B, 2 * KEY_B + VAL_B + PIECE_N, N_B + V7X_LANES)
    proj_v = [proj(pb_ref, COL_B, 2 * KEY_B + lo, 2 * KEY_B + hi) for lo, hi in pieces(VAL_B)]
    proj_z = proj(pb_ref, COL_B, 2 * KEY_B + VAL_B, 2 * KEY_B + VAL_B + PIECE_N)
    gates = [gate(lo, hi) for lo, hi in pieces(N_G)]
    mix = [mixa_mix(g) for g in range(H_A)]
    rows = range(N_CB)

    order = [norm_prev(r) for r in rows]
    order += [proj_a[2], proj_a[3]] + [act_v(r) for r in rows]
    order += [proj_a[0], proj_a[1]] + [act_u(r) for r in rows]
    order += [proj_a[4], proj_a[5]] + [act_z(r) for r in rows]
    order += [proj_qk, proj_zl, decay_gate, proj_v[0], proj_v[1], decay_cumsum]
    order += [gla_scale(c) for c in rows]
    order += [proj_z]
    n_pairs = N_CB // 2
    gates_per, mix_per = len(gates) // n_pairs, len(mix) // N_CB
    for j in range(n_pairs):
        order += [gla_scores(2 * j), gla_scores(2 * j + 1)] + gates[j * gates_per:(j + 1) * gates_per]
        for c in (2 * j, 2 * j + 1):
            order += [gla_out(c)] + mix[c * mix_per:(c + 1) * mix_per]
    order += [norm_next(r) for r in rows]
    order += [branch_a(lo, hi) for lo, hi in pieces(D_MODEL)]
    order += [branch_b(0), branch_b(1), out_proj(0), out_proj(1)]
    for task in order:
        task()


def _const_spec(shape):
    nd = len(shape)
    return pl.BlockSpec(shape, lambda t: (0,) * nd, pipeline_mode=pl.Buffered(1))


N_STAGE_SLOTS = 2
N_WEIGHT_STREAMS = 4

_WEIGHT_SCRATCH = (
    ((D_MODEL, W_IN_COLS), jnp.bfloat16),
    ((D_MODEL, W_SQ_COLS), jnp.bfloat16),
    ((N_STAGE_SLOTS, STAGE_COLS_IN, D_MODEL), jnp.float32),
    ((N_STAGE_SLOTS, 3, STAGE_ROWS, D_MODEL), jnp.float32),
)

_SCRATCH = (
    ((H_B, DK_B, DV_B), jnp.float32),
    ((TILE_S, D_MODEL), jnp.bfloat16),
    ((TILE_S, D_MODEL), jnp.float32),
    ((TILE_S, N_A), jnp.float32),
    ((TILE_S, N_B + V7X_LANES), jnp.float32),
    ((TILE_S, N_G), jnp.float32),
    ((TILE_S, D_A), jnp.bfloat16),
    ((TILE_S, D_A), jnp.float32),
    ((TILE_S, KEY_B), jnp.bfloat16),
    ((TILE_S, KEY_B), jnp.bfloat16),
    ((TILE_S, KEY_B), jnp.float32),
    ((TILE_S, KEY_B), jnp.bfloat16),
    ((TILE_S, KEY_B), jnp.bfloat16),
    ((TILE_S, KEY_B), jnp.bfloat16),
    ((TILE_S, KEY_B), jnp.float32),
    ((TILE_S, D_MODEL), jnp.bfloat16),
    ((TILE_S, D_MODEL), jnp.bfloat16),
    ((TILE_S, D_MODEL), jnp.float32),
    ((TILE_S, D_MODEL), jnp.bfloat16),
)


def _vmem_limit_bytes():
    nbytes = lambda shape, dt: math.prod(shape) * jnp.dtype(dt).itemsize
    params = 2 * V7X_LANES * KEY_B + 4 * (H_A * CHUNK_A * CHUNK_A + CHUNK_A * D_A)
    io_windows = 3 * 2 * TILE_S * D_MODEL * 4
    scratch = sum(nbytes(s, d) for s, d in _WEIGHT_SCRATCH + _SCRATCH)
    temporaries = 12 * TILE_S * D_MODEL * 4
    return min(params + io_windows + scratch + temporaries, V7X_VMEM_BYTES - 8 * 1024 * 1024)


@jax.jit
def _forward(x, norm_g, w_in, ln_v_g, ln_v_b, w_spatial, b_spatial, w_gate_up, b_gate_up,
             gla_norm_g, w_branch_a, w_branch_b, w_out, final_norm_g):
    bsz, seq, d = x.shape
    assert d == D_MODEL and seq % TILE_S == 0
    assert TILE_S % CHUNK_A == 0 and TILE_S % CHUNK_B == 0
    assert w_in.shape == (D_MODEL, SRC_G + N_G)
    w_gu = jnp.pad(w_gate_up, ((0, V7X_LANES - GATE_RANK), (0, 0))).astype(jnp.bfloat16)
    b_s = jnp.repeat(b_spatial.T, GROUP_A, axis=1)
    row = lambda a: a.reshape(1, -1)

    hbm_weights = (w_in.T, w_branch_a, w_branch_b, w_out)
    params = (row(ln_v_g), row(ln_v_b), w_spatial, b_s, w_gu, row(b_gate_up), row(gla_norm_g),
              row(final_norm_g))
    tiles_per_row = seq // TILE_S
    n_tiles = bsz * tiles_per_row

    def tile_spec(offset):
        def index_map(t):
            tile = jnp.clip(t + offset, 0, n_tiles - 1)
            return tile // tiles_per_row, lax.rem(tile, tiles_per_row), 0
        return pl.BlockSpec((None, TILE_S, D_MODEL), index_map)

    operands = (x, x, row(norm_g)) + hbm_weights + params
    in_specs = [tile_spec(0), tile_spec(1), _const_spec((1, D_MODEL))]
    in_specs += [pl.BlockSpec(memory_space=pl.ANY) for _ in hbm_weights]
    in_specs += [_const_spec(op.shape) for op in params]
    return pl.pallas_call(
        functools.partial(_layer_kernel, tiles_per_row),
        grid=(n_tiles + 1,),
        in_specs=in_specs,
        out_specs=tile_spec(-1),
        out_shape=jax.ShapeDtypeStruct(x.shape, x.dtype),
        scratch_shapes=[pltpu.VMEM(s, d) for s, d in _WEIGHT_SCRATCH]
        + [pltpu.SemaphoreType.DMA((N_STAGE_SLOTS, N_WEIGHT_STREAMS))]
        + [pltpu.VMEM(s, d) for s, d in _SCRATCH],
        compiler_params=pltpu.CompilerParams(
            dimension_semantics=("arbitrary",),
            vmem_limit_bytes=_vmem_limit_bytes(),
        ),
        name="hybrid_layer",
    )(*operands)


def kernel(x, norm_g, w_in, ln_v_g, ln_v_b, w_spatial, b_spatial, w_gate_up, b_gate_up, gla_norm_g,
           w_branch_a, w_branch_b, w_out, final_norm_g):
    assert norm_g.shape[0] == 1, "single-layer problem"
    return _forward(x, norm_g[0], w_in[0], ln_v_g[0], ln_v_b[0], w_spatial[0], b_spatial[0],
                    w_gate_up[0], b_gate_up[0], gla_norm_g[0], w_branch_a[0], w_branch_b[0],
                    w_out[0], final_norm_g)
```

```python
import math

import jax
import jax.numpy as jnp
from jax import lax
from jax.experimental import pallas as pl
from jax.experimental.pallas import tpu as pltpu

D_MODEL = 1024
D_A = D_MODEL
H_A = 8
GROUP_A = D_A // H_A
CHUNK_A = 128
H_B = 4
KEY_B = D_MODEL // 2
VAL_B = D_MODEL
DK_B = KEY_B // H_B
DV_B = VAL_B // H_B
GATE_RANK = 16
GATE_NORM = 16.0
CHUNK_B = 64
EPS = 1e-6
LN_EPS = 1e-5

V7X_LANES = 128
V7X_SUBLANES = 8
V7X_VMEM_BYTES = 64 * 1024 * 1024
TILE_S = 256
PIECE_N = 512
STAGE_ROWS = 64
STAGE_COLS_IN = 512

N_A = 3 * D_A
N_B = 2 * KEY_B + 2 * VAL_B
N_G = 2 * D_MODEL
N_CA = TILE_S // CHUNK_A
N_CB = TILE_S // CHUNK_B
SRC_G = N_A + N_B + GATE_RANK

COL_A = 0
COL_B = COL_A + N_A
COL_LR = COL_B + N_B
COL_G = COL_LR + V7X_LANES
W_IN_COLS = COL_G + N_G
COL_BA = 0
COL_BB = COL_BA + D_MODEL
COL_OUT = COL_BB + D_MODEL
W_SQ_COLS = COL_OUT + D_MODEL + V7X_LANES
assert (W_IN_COLS // V7X_LANES) % 2 == 1 and (W_SQ_COLS // V7X_LANES) % 2 == 1
assert COL_LR % STAGE_COLS_IN == 0 and N_G % STAGE_COLS_IN == 0


def _bf(x):
    return x.astype(jnp.bfloat16)


def _dot(a, b):
    return jnp.dot(a, b, preferred_element_type=jnp.float32)


def _dot_nt(a, b):
    return lax.dot_general(a, b, (((1,), (1,)), ((), ())), preferred_element_type=jnp.float32)


def _gelu(x):
    return 0.5 * x * (1.0 + lax.erf(x * (1.0 / math.sqrt(2.0))))


def _sigmoid(x):
    return 1.0 / (1.0 + jnp.exp(-x))


def _silu(x):
    return x * _sigmoid(x)


def _log_sigmoid(x):
    return jnp.minimum(x, 0.0) - jnp.log1p(jnp.exp(-jnp.abs(x)))


def _rms(x, g):
    return x * lax.rsqrt(jnp.mean(x * x, axis=-1, keepdims=True) + EPS) * g


def _stage_chunks(n_chunks, copies, store):
    for cp in copies(0, 0):
        cp.start()

    def body(i, carry):
        slot = lax.rem(i, N_STAGE_SLOTS)

        @pl.when(i + 1 < n_chunks)
        def _():
            for cp in copies(i + 1, lax.rem(i + 1, N_STAGE_SLOTS)):
                cp.start()

        for cp in copies(i, slot):
            cp.wait()
        store(i, slot)
        return carry

    lax.fori_loop(0, n_chunks, body, 0)


def _stage_weights(w_in_t_hbm, w_sq_hbm, w_in_ref, w_sq_ref, st_in_ref, st_sq_ref, sem_ref):
    def in_chunks(src0, dst0):
        def copies(i, slot):
            rows = pl.ds(pl.multiple_of(src0 + i * STAGE_COLS_IN, 16), STAGE_COLS_IN)
            return [pltpu.make_async_copy(w_in_t_hbm.at[rows, :], st_in_ref.at[slot], sem_ref.at[slot, 0])]

        def store(i, slot):
            cols = pl.ds(pl.multiple_of(dst0 + i * STAGE_COLS_IN, STAGE_COLS_IN), STAGE_COLS_IN)
            w_in_ref[:, cols] = _bf(st_in_ref[slot].T)
        return copies, store

    def sq_copies(i, slot):
        rows = pl.ds(pl.multiple_of(i * STAGE_ROWS, STAGE_ROWS), STAGE_ROWS)
        return [pltpu.make_async_copy(w.at[rows, :], st_sq_ref.at[slot, k], sem_ref.at[slot, 1 + k])
                for k, w in enumerate(w_sq_hbm)]

    def sq_store(i, slot):
        rows = pl.ds(pl.multiple_of(i * STAGE_ROWS, STAGE_ROWS), STAGE_ROWS)
        for k in range(len(w_sq_hbm)):
            w_sq_ref[rows, k * D_MODEL:(k + 1) * D_MODEL] = _bf(st_sq_ref[slot, k])

    _stage_chunks(COL_LR // STAGE_COLS_IN, *in_chunks(0, 0))
    _stage_chunks(N_G // STAGE_COLS_IN, *in_chunks(SRC_G, COL_G))
    _stage_chunks(D_MODEL // STAGE_ROWS, sq_copies, sq_store)
    lr_copy = pltpu.make_async_copy(w_in_t_hbm.at[pl.ds(COL_LR, GATE_RANK), :],
                                    st_in_ref.at[0, pl.ds(0, GATE_RANK), :], sem_ref.at[0, 0])
    lr_copy.start()
    lr_copy.wait()
    w_in_ref[:, COL_LR:COL_G] = jnp.zeros((D_MODEL, V7X_LANES), jnp.bfloat16)
    w_in_ref[:, COL_LR:COL_LR + GATE_RANK] = _bf(st_in_ref[0, 0:GATE_RANK, :].T)


def _layer_kernel(x_ref, norm_g_ref, w_in_hbm, w_ba_hbm, w_bb_hbm, w_out_hbm,
                  ln_g_ref, ln_b_ref, w_s_ref, b_s_ref, w_gu_ref, b_gu_ref, gla_g_ref, fin_g_ref,
                  out_ref,
                  w_in_ref, w_sq_ref, st_in_ref, st_sq_ref, sem_ref,
                  state_ref, hb_ref, pa_ref, pb_ref, gate_ref, vn_ref, ga_ref, gh_ref, gl_ref, b_ref,
                  qi_ref, ki_ref, qs_ref, ks_ref, a_ref, o_ref, ba_ref, mg_ref):
    @pl.when((pl.program_id(0) == 0) & (pl.program_id(1) == 0))
    def _():
        _stage_weights(w_in_hbm, (w_ba_hbm, w_bb_hbm, w_out_hbm), w_in_ref, w_sq_ref,
                       st_in_ref, st_sq_ref, sem_ref)

    @pl.when(pl.program_id(1) == 0)
    def _():
        state_ref[...] = jnp.zeros_like(state_ref)

    def norm_in():
        hb_ref[...] = _bf(_rms(x_ref[...], norm_g_ref[...]))

    def proj(dst_ref, col0, lo, hi):
        def task():
            dst_ref[:, lo:hi] = _dot(hb_ref[...], w_in_ref[:, col0 + lo:col0 + hi])
        return task

    def act_v(r):
        def task():
            rs = slice(r * CHUNK_B, (r + 1) * CHUNK_B)
            v = _gelu(pa_ref[rs, D_A:2 * D_A])
            mu = jnp.mean(v, axis=-1, keepdims=True)
            vc = v - mu
            inv = lax.rsqrt(jnp.mean(vc * vc, axis=-1, keepdims=True) + LN_EPS)
            vn_ref[rs, :] = _bf(vc * inv * ln_g_ref[...] + ln_b_ref[...])
        return task

    def act_u(r):
        def task():
            rs = slice(r * CHUNK_B, (r + 1) * CHUNK_B)
            ga_ref[rs, :] = _gelu(pa_ref[rs, 0:D_A])
        return task

    def act_z(r):
        def task():
            rs = slice(r * CHUNK_B, (r + 1) * CHUNK_B)
            ga_ref[rs, :] = ga_ref[rs, :] * _silu(pa_ref[rs, 2 * D_A:3 * D_A])
        return task

    def mixa_mix(g):
        def task():
            row = lax.broadcasted_iota(jnp.int32, (CHUNK_A, CHUNK_A), 0)
            col = lax.broadcasted_iota(jnp.int32, (CHUNK_A, CHUNK_A), 1)
            w_g = _bf(jnp.where(row >= col, w_s_ref[g], 0.0))
            cs = slice(g * GROUP_A, (g + 1) * GROUP_A)
            chunks = [slice(c * CHUNK_A, (c + 1) * CHUNK_A) for c in range(N_CA)]
            mixed = _dot(w_g, jnp.concatenate([vn_ref[rs, cs] for rs in chunks], axis=1))
            for c, rs in enumerate(chunks):
                m_c = mixed[:, c * GROUP_A:(c + 1) * GROUP_A] + b_s_ref[:, cs]
                a_ref[rs, cs] = _bf(ga_ref[rs, cs] * m_c)
        return task

    def decay_gate():
        logit = _dot(_bf(pb_ref[:, N_B:N_B + V7X_LANES]), w_gu_ref[...]) + b_gu_ref[...]
        log_alpha = _log_sigmoid(logit) * (1.0 / GATE_NORM)
        g_hi = _bf(log_alpha)
        gh_ref[...] = g_hi
        gl_ref[...] = _bf(log_alpha - g_hi.astype(jnp.float32))

    def decay_cumsum():
        ri = lax.broadcasted_iota(jnp.int32, (TILE_S, TILE_S), 0)
        ci = lax.broadcasted_iota(jnp.int32, (TILE_S, TILE_S), 1)
        tril = _bf(jnp.where((ri >= ci) & (ri // CHUNK_B == ci // CHUNK_B), 1.0, 0.0))
        b_ref[...] = _dot(tril, gh_ref[...]) + _dot(tril, gl_ref[...])

    def gla_scale(c):
        def task():
            rs = slice(c * CHUNK_B, (c + 1) * CHUNK_B)
            b = b_ref[rs, :]
            b_mid = b[CHUNK_B // 2 - 1:CHUNK_B // 2]
            b_last = b[CHUNK_B - 1:CHUNK_B]
            q = pb_ref[rs, 0:KEY_B] * (DK_B ** -0.5)
            k = pb_ref[rs, KEY_B:2 * KEY_B]
            qi_ref[rs, :] = _bf(q * jnp.exp(b - b_mid))
            ki_ref[rs, :] = _bf(k * jnp.exp(b_mid - b))
            qs_ref[rs, :] = _bf(q * jnp.exp(b))
            ks_ref[rs, :] = k * jnp.exp(b_last - b)
        return task

    scores = {}

    def gla_scores(c):
        def task():
            rs = slice(c * CHUNK_B, (c + 1) * CHUNK_B)
            rc = lax.broadcasted_iota(jnp.int32, (CHUNK_B, CHUNK_B), 0)
            cc = lax.broadcasted_iota(jnp.int32, (CHUNK_B, CHUNK_B), 1)
            for h in range(H_B):
                ks = slice(h * DK_B, (h + 1) * DK_B)
                scores[c, h] = _bf(jnp.where(rc >= cc, _dot_nt(qi_ref[rs, ks], ki_ref[rs, ks]), 0.0))
        return task

    def gla_out(c):
        def task():
            rs = slice(c * CHUNK_B, (c + 1) * CHUNK_B)
            for h in range(H_B):
                ks = slice(h * DK_B, (h + 1) * DK_B)
                vs = slice(h * DV_B, (h + 1) * DV_B)
                st = state_ref[h]
                v_h = _bf(pb_ref[rs, 2 * KEY_B + h * DV_B:2 * KEY_B + (h + 1) * DV_B])
                o_h = _dot(scores[c, h], v_h) + _dot(qs_ref[rs, ks], _bf(st))
                decay_row = jnp.exp(b_ref[(c + 1) * CHUNK_B - 1:(c + 1) * CHUNK_B, ks])
                decay_col = jnp.broadcast_to(decay_row, (V7X_SUBLANES, DK_B)).T[:, 0:1]
                state_ref[h] = st * decay_col + _dot(_bf(ks_ref[rs, ks].T), v_h)
                z_b = pb_ref[rs, 2 * KEY_B + VAL_B + h * DV_B:2 * KEY_B + VAL_B + (h + 1) * DV_B]
                o_ref[rs, vs] = _bf(_rms(o_h, gla_g_ref[...]) * _silu(z_b))
        return task

    def gate(lo, hi):
        def task():
            gate_ref[:, lo:hi] = _sigmoid(_dot(hb_ref[...], w_in_ref[:, COL_G + lo:COL_G + hi]))
        return task

    def branch_a(lo, hi):
        def task():
            ba_ref[:, lo:hi] = gate_ref[:, lo:hi] * _dot(a_ref[...], w_sq_ref[:, COL_BA + lo:COL_BA + hi])
        return task

    half = TILE_S // 2

    def branch_b(r):
        def task():
            rs = slice(r * half, (r + 1) * half)
            p_b = _dot(o_ref[rs, :], w_sq_ref[:, COL_BB:COL_BB + D_MODEL])
            mg_ref[rs, :] = _bf(ba_ref[rs, :] + gate_ref[rs, D_MODEL:] * p_b)
        return task

    def out_proj(r):
        def task():
            rs = slice(r * half, (r + 1) * half)
            y = x_ref[rs, :] + _dot(mg_ref[rs, :], w_sq_ref[:, COL_OUT:COL_OUT + D_MODEL])
            out_ref[rs, :] = _rms(y, fin_g_ref[...])
        return task

    pieces = lambda n: [(lo, min(lo + PIECE_N, n)) for lo in range(0, n, PIECE_N)]
    proj_a = [proj(pa_ref, COL_A, lo, hi) for lo, hi in pieces(N_A)]
    proj_qk = proj(pb_ref, COL_B, 0, 2 * KEY_B)
    proj_zl = proj(pb_ref, COL_B, 2 * KEY_B + VAL_B + PIECE_N, N_B + V7X_LANES)
    proj_v = [proj(pb_ref, COL_B, 2 * KEY_B + lo, 2 * KEY_B + hi) for lo, hi in pieces(VAL_B)]
    proj_z = proj(pb_ref, COL_B, 2 * KEY_B + VAL_B, 2 * KEY_B + VAL_B + PIECE_N)
    gates = [gate(lo, hi) for lo, hi in pieces(N_G)]
    mix = [mixa_mix(g) for g in range(H_A)]
    rows = range(N_CB)

    order = [norm_in, proj_a[2], proj_a[3]] + [act_v(r) for r in rows]
    order += [proj_zl, proj_qk, decay_gate]
    order += [proj_a[0], proj_a[1]] + [act_u(r) for r in rows]
    order += [proj_v[0], proj_v[1], decay_cumsum] + [gla_scale(c) for c in rows]
    order += [proj_a[4], proj_a[5]] + [act_z(r) for r in rows]
    order += [proj_z]
    n_pairs = N_CB // 2
    gates_per, mix_per = len(gates) // n_pairs, len(mix) // N_CB
    for j in range(n_pairs):
        order += [gla_scores(2 * j), gla_scores(2 * j + 1)] + gates[j * gates_per:(j + 1) * gates_per]
        for c in (2 * j, 2 * j + 1):
            order += [gla_out(c)] + mix[c * mix_per:(c + 1) * mix_per]
    order += [branch_a(lo, hi) for lo, hi in pieces(D_MODEL)]
    order += [branch_b(0), branch_b(1), out_proj(0), out_proj(1)]
    for task in order:
        task()


def _const_spec(shape):
    nd = len(shape)
    return pl.BlockSpec(shape, lambda b, s: (0,) * nd, pipeline_mode=pl.Buffered(1))


N_STAGE_SLOTS = 2
N_WEIGHT_STREAMS = 4

_WEIGHT_SCRATCH = (
    ((D_MODEL, W_IN_COLS), jnp.bfloat16),
    ((D_MODEL, W_SQ_COLS), jnp.bfloat16),
    ((N_STAGE_SLOTS, STAGE_COLS_IN, D_MODEL), jnp.float32),
    ((N_STAGE_SLOTS, 3, STAGE_ROWS, D_MODEL), jnp.float32),
)

_SCRATCH = (
    ((H_B, DK_B, DV_B), jnp.float32),
    ((TILE_S, D_MODEL), jnp.bfloat16),
    ((TILE_S, N_A), jnp.float32),
    ((TILE_S, N_B + V7X_LANES), jnp.float32),
    ((TILE_S, N_G), jnp.float32),
    ((TILE_S, D_A), jnp.bfloat16),
    ((TILE_S, D_A), jnp.float32),
    ((TILE_S, KEY_B), jnp.bfloat16),
    ((TILE_S, KEY_B), jnp.bfloat16),
    ((TILE_S, KEY_B), jnp.float32),
    ((TILE_S, KEY_B), jnp.bfloat16),
    ((TILE_S, KEY_B), jnp.bfloat16),
    ((TILE_S, KEY_B), jnp.bfloat16),
    ((TILE_S, KEY_B), jnp.float32),
    ((TILE_S, D_MODEL), jnp.bfloat16),
    ((TILE_S, D_MODEL), jnp.bfloat16),
    ((TILE_S, D_MODEL), jnp.float32),
    ((TILE_S, D_MODEL), jnp.bfloat16),
)


def _vmem_limit_bytes():
    nbytes = lambda shape, dt: math.prod(shape) * jnp.dtype(dt).itemsize
    params = 2 * V7X_LANES * KEY_B + 4 * (H_A * CHUNK_A * CHUNK_A + CHUNK_A * D_A)
    io_windows = 2 * 2 * TILE_S * D_MODEL * 4
    scratch = sum(nbytes(s, d) for s, d in _WEIGHT_SCRATCH + _SCRATCH)
    temporaries = 12 * TILE_S * D_MODEL * 4
    return min(params + io_windows + scratch + temporaries, V7X_VMEM_BYTES - 8 * 1024 * 1024)


@jax.jit
def _forward(x, norm_g, w_in, ln_v_g, ln_v_b, w_spatial, b_spatial, w_gate_up, b_gate_up,
             gla_norm_g, w_branch_a, w_branch_b, w_out, final_norm_g):
    bsz, seq, d = x.shape
    assert d == D_MODEL and seq % TILE_S == 0
    assert TILE_S % CHUNK_A == 0 and TILE_S % CHUNK_B == 0
    assert w_in.shape == (D_MODEL, SRC_G + N_G)
    w_gu = jnp.pad(w_gate_up, ((0, V7X_LANES - GATE_RANK), (0, 0))).astype(jnp.bfloat16)
    b_s = jnp.repeat(b_spatial.T, GROUP_A, axis=1)
    row = lambda a: a.reshape(1, -1)

    hbm_weights = (w_in.T, w_branch_a, w_branch_b, w_out)
    params = (row(ln_v_g), row(ln_v_b), w_spatial, b_s, w_gu, row(b_gate_up), row(gla_norm_g),
              row(final_norm_g))
    operands = (x, row(norm_g)) + hbm_weights + params
    in_specs = [pl.BlockSpec((None, TILE_S, D_MODEL), lambda b, s: (b, s, 0)), _const_spec((1, D_MODEL))]
    in_specs += [pl.BlockSpec(memory_space=pl.ANY) for _ in hbm_weights]
    in_specs += [_const_spec(op.shape) for op in params]
    return pl.pallas_call(
        _layer_kernel,
        grid=(bsz, seq // TILE_S),
        in_specs=in_specs,
        out_specs=pl.BlockSpec((None, TILE_S, D_MODEL), lambda b, s: (b, s, 0)),
        out_shape=jax.ShapeDtypeStruct(x.shape, x.dtype),
        scratch_shapes=[pltpu.VMEM(s, d) for s, d in _WEIGHT_SCRATCH]
        + [pltpu.SemaphoreType.DMA((N_STAGE_SLOTS, N_WEIGHT_STREAMS))]
        + [pltpu.VMEM(s, d) for s, d in _SCRATCH],
        compiler_params=pltpu.CompilerParams(
            dimension_semantics=("arbitrary", "arbitrary"),
            vmem_limit_bytes=_vmem_limit_bytes(),
        ),
        name="hybrid_layer",
    )(*operands)


def kernel(x, norm_g, w_in, ln_v_g, ln_v_b, w_spatial, b_spatial, w_gate_up, b_gate_up, gla_norm_g,
           w_branch_a, w_branch_b, w_out, final_norm_g):
    assert norm_g.shape[0] == 1, "single-layer problem"
    return _forward(x, norm_g[0], w_in[0], ln_v_g[0], ln_v_b[0], w_spatial[0], b_spatial[0],
                    w_gate_up[0], b_gate_up[0], gla_norm_g[0], w_branch_a[0], w_branch_b[0],
                    w_out[0], final_norm_g)
```

```python
import functools
import math

import jax
import jax.numpy as jnp
from jax import lax
from jax.experimental import pallas as pl
from jax.experimental.pallas import tpu as pltpu

D_MODEL = 1024
D_A = D_MODEL
H_A = 8
GROUP_A = D_A // H_A
CHUNK_A = 128
H_B = 4
KEY_B = D_MODEL // 2
VAL_B = D_MODEL
DK_B = KEY_B // H_B
DV_B = VAL_B // H_B
GATE_RANK = 16
GATE_NORM = 16.0
CHUNK_B = 64
EPS = 1e-6
LN_EPS = 1e-5

V7X_LANES = 128
V7X_SUBLANES = 8
V7X_VMEM_BYTES = 64 * 1024 * 1024
TILE_S = 256
PIECE_N = 512
STAGE_ROWS = 64
STAGE_COLS_IN = 512

N_A = 3 * D_A
N_B = 2 * KEY_B + 2 * VAL_B
N_G = 2 * D_MODEL
N_CA = TILE_S // CHUNK_A
N_CB = TILE_S // CHUNK_B
SRC_G = N_A + N_B + GATE_RANK

COL_A = 0
COL_B = COL_A + N_A
COL_LR = COL_B + N_B
COL_G = COL_LR + V7X_LANES
W_IN_COLS = COL_G + N_G
COL_BA = 0
COL_BB = COL_BA + D_MODEL
COL_OUT = COL_BB + D_MODEL
W_SQ_COLS = COL_OUT + D_MODEL + V7X_LANES
assert (W_IN_COLS // V7X_LANES) % 2 == 1 and (W_SQ_COLS // V7X_LANES) % 2 == 1
assert COL_LR % STAGE_COLS_IN == 0 and N_G % STAGE_COLS_IN == 0


def _bf(x):
    return x.astype(jnp.bfloat16)


def _dot(a, b):
    return jnp.dot(a, b, preferred_element_type=jnp.float32)


def _dot_nt(a, b):
    return lax.dot_general(a, b, (((1,), (1,)), ((), ())), preferred_element_type=jnp.float32)


def _gelu(x):
    return 0.5 * x * (1.0 + lax.erf(x * (1.0 / math.sqrt(2.0))))


def _sigmoid(x):
    return 1.0 / (1.0 + jnp.exp(-x))


def _silu(x):
    return x * _sigmoid(x)


def _log_sigmoid(x):
    return jnp.minimum(x, 0.0) - jnp.log1p(jnp.exp(-jnp.abs(x)))


def _rms(x, g):
    return x * lax.rsqrt(jnp.mean(x * x, axis=-1, keepdims=True) + EPS) * g


def _stage_chunks(n_chunks, copies, store):
    for cp in copies(0, 0):
        cp.start()

    def body(i, carry):
        slot = lax.rem(i, N_STAGE_SLOTS)

        @pl.when(i + 1 < n_chunks)
        def _():
            for cp in copies(i + 1, lax.rem(i + 1, N_STAGE_SLOTS)):
                cp.start()

        for cp in copies(i, slot):
            cp.wait()
        store(i, slot)
        return carry

    lax.fori_loop(0, n_chunks, body, 0)


def _stage_weights(w_in_t_hbm, w_sq_hbm, w_in_ref, w_sq_ref, st_in_ref, st_sq_ref, sem_ref):
    def in_chunks(src0, dst0):
        def copies(i, slot):
            rows = pl.ds(pl.multiple_of(src0 + i * STAGE_COLS_IN, 16), STAGE_COLS_IN)
            return [pltpu.make_async_copy(w_in_t_hbm.at[rows, :], st_in_ref.at[slot], sem_ref.at[slot, 0])]

        def store(i, slot):
            cols = pl.ds(pl.multiple_of(dst0 + i * STAGE_COLS_IN, STAGE_COLS_IN), STAGE_COLS_IN)
            w_in_ref[:, cols] = _bf(st_in_ref[slot].T)
        return copies, store

    def sq_copies(i, slot):
        rows = pl.ds(pl.multiple_of(i * STAGE_ROWS, STAGE_ROWS), STAGE_ROWS)
        return [pltpu.make_async_copy(w.at[rows, :], st_sq_ref.at[slot, k], sem_ref.at[slot, 1 + k])
                for k, w in enumerate(w_sq_hbm)]

    def sq_store(i, slot):
        rows = pl.ds(pl.multiple_of(i * STAGE_ROWS, STAGE_ROWS), STAGE_ROWS)
        for k in range(len(w_sq_hbm)):
            w_sq_ref[rows, k * D_MODEL:(k + 1) * D_MODEL] = _bf(st_sq_ref[slot, k])

    _stage_chunks(COL_LR // STAGE_COLS_IN, *in_chunks(0, 0))
    _stage_chunks(N_G // STAGE_COLS_IN, *in_chunks(SRC_G, COL_G))
    _stage_chunks(D_MODEL // STAGE_ROWS, sq_copies, sq_store)
    lr_copy = pltpu.make_async_copy(w_in_t_hbm.at[pl.ds(COL_LR, GATE_RANK), :],
                                    st_in_ref.at[0, pl.ds(0, GATE_RANK), :], sem_ref.at[0, 0])
    lr_copy.start()
    lr_copy.wait()
    w_in_ref[:, COL_LR:COL_G] = jnp.zeros((D_MODEL, V7X_LANES), jnp.bfloat16)
    w_in_ref[:, COL_LR:COL_LR + GATE_RANK] = _bf(st_in_ref[0, 0:GATE_RANK, :].T)


def _layer_kernel(tiles_per_row,
                  x_ref, norm_g_ref, w_in_hbm, w_ba_hbm, w_bb_hbm, w_out_hbm,
                  ln_g_ref, ln_b_ref, w_s_ref, b_s_ref, w_gu_ref, b_gu_ref, gla_g_ref, fin_g_ref,
                  out_ref,
                  w_in_ref, w_sq_ref, st_in_ref, st_sq_ref, sem_ref,
                  state_ref, hb_ref, xk_ref, pa_ref, pb_ref, gate_ref, vn_ref, ga_ref, gh_ref, gl_ref, b_ref,
                  qi_ref, ki_ref, qs_ref, ks_ref, a_ref, o_ref, ba_ref, mg_ref):
    step = pl.program_id(0)

    @pl.when(step == 0)
    def _():
        _stage_weights(w_in_hbm, (w_ba_hbm, w_bb_hbm, w_out_hbm), w_in_ref, w_sq_ref,
                       st_in_ref, st_sq_ref, sem_ref)
        mg_ref[...] = jnp.zeros_like(mg_ref)
        xk_ref[...] = jnp.zeros_like(xk_ref)

    @pl.when(lax.rem(step, tiles_per_row) == 0)
    def _():
        state_ref[...] = jnp.zeros_like(state_ref)

    def norm_in():
        x = x_ref[...]
        hb_ref[...] = _bf(_rms(x, norm_g_ref[...]))

    def keep_x():
        xk_ref[...] = x_ref[...]

    def proj(dst_ref, col0, lo, hi):
        def task():
            dst_ref[:, lo:hi] = _dot(hb_ref[...], w_in_ref[:, col0 + lo:col0 + hi])
        return task

    def act_v(r):
        def task():
            rs = slice(r * CHUNK_B, (r + 1) * CHUNK_B)
            v = _gelu(pa_ref[rs, D_A:2 * D_A])
            mu = jnp.mean(v, axis=-1, keepdims=True)
            vc = v - mu
            inv = lax.rsqrt(jnp.mean(vc * vc, axis=-1, keepdims=True) + LN_EPS)
            vn_ref[rs, :] = _bf(vc * inv * ln_g_ref[...] + ln_b_ref[...])
        return task

    def act_u(r):
        def task():
            rs = slice(r * CHUNK_B, (r + 1) * CHUNK_B)
            ga_ref[rs, :] = _gelu(pa_ref[rs, 0:D_A])
        return task

    def act_z(r):
        def task():
            rs = slice(r * CHUNK_B, (r + 1) * CHUNK_B)
            ga_ref[rs, :] = ga_ref[rs, :] * _silu(pa_ref[rs, 2 * D_A:3 * D_A])
        return task

    def mixa_mix(g):
        def task():
            row = lax.broadcasted_iota(jnp.int32, (CHUNK_A, CHUNK_A), 0)
            col = lax.broadcasted_iota(jnp.int32, (CHUNK_A, CHUNK_A), 1)
            w_g = _bf(jnp.where(row >= col, w_s_ref[g], 0.0))
            cs = slice(g * GROUP_A, (g + 1) * GROUP_A)
            chunks = [slice(c * CHUNK_A, (c + 1) * CHUNK_A) for c in range(N_CA)]
            mixed = _dot(w_g, jnp.concatenate([vn_ref[rs, cs] for rs in chunks], axis=1))
            for c, rs in enumerate(chunks):
                m_c = mixed[:, c * GROUP_A:(c + 1) * GROUP_A] + b_s_ref[:, cs]
                a_ref[rs, cs] = _bf(ga_ref[rs, cs] * m_c)
        return task

    def decay_gate():
        logit = _dot(_bf(pb_ref[:, N_B:N_B + V7X_LANES]), w_gu_ref[...]) + b_gu_ref[...]
        log_alpha = _log_sigmoid(logit) * (1.0 / GATE_NORM)
        g_hi = _bf(log_alpha)
        gh_ref[...] = g_hi
        gl_ref[...] = _bf(log_alpha - g_hi.astype(jnp.float32))

    def decay_cumsum():
        ri = lax.broadcasted_iota(jnp.int32, (TILE_S, TILE_S), 0)
        ci = lax.broadcasted_iota(jnp.int32, (TILE_S, TILE_S), 1)
        tril = _bf(jnp.where((ri >= ci) & (ri // CHUNK_B == ci // CHUNK_B), 1.0, 0.0))
        b_ref[...] = _dot(tril, gh_ref[...]) + _dot(tril, gl_ref[...])

    def gla_scale(c):
        def task():
            rs = slice(c * CHUNK_B, (c + 1) * CHUNK_B)
            b = b_ref[rs, :]
            b_mid = b[CHUNK_B // 2 - 1:CHUNK_B // 2]
            b_last = b[CHUNK_B - 1:CHUNK_B]
            q = pb_ref[rs, 0:KEY_B] * (DK_B ** -0.5)
            k = pb_ref[rs, KEY_B:2 * KEY_B]
            qi_ref[rs, :] = _bf(q * jnp.exp(b - b_mid))
            ki_ref[rs, :] = _bf(k * jnp.exp(b_mid - b))
            qs_ref[rs, :] = _bf(q * jnp.exp(b))
            ks_ref[rs, :] = k * jnp.exp(b_last - b)
        return task

    scores = {}

    def gla_scores(c):
        def task():
            rs = slice(c * CHUNK_B, (c + 1) * CHUNK_B)
            rc = lax.broadcasted_iota(jnp.int32, (CHUNK_B, CHUNK_B), 0)
            cc = lax.broadcasted_iota(jnp.int32, (CHUNK_B, CHUNK_B), 1)
            for h in range(H_B):
                ks = slice(h * DK_B, (h + 1) * DK_B)
                scores[c, h] = _bf(jnp.where(rc >= cc, _dot_nt(qi_ref[rs, ks], ki_ref[rs, ks]), 0.0))
        return task

    def gla_out(c):
        def task():
            rs = slice(c * CHUNK_B, (c + 1) * CHUNK_B)
            for h in range(H_B):
                ks = slice(h * DK_B, (h + 1) * DK_B)
                vs = slice(h * DV_B, (h + 1) * DV_B)
                st = state_ref[h]
                v_h = _bf(pb_ref[rs, 2 * KEY_B + h * DV_B:2 * KEY_B + (h + 1) * DV_B])
                o_h = _dot(scores[c, h], v_h) + _dot(qs_ref[rs, ks], _bf(st))
                decay_row = jnp.exp(b_ref[(c + 1) * CHUNK_B - 1:(c + 1) * CHUNK_B, ks])
                decay_col = jnp.broadcast_to(decay_row, (V7X_SUBLANES, DK_B)).T[:, 0:1]
                state_ref[h] = st * decay_col + _dot(_bf(ks_ref[rs, ks].T), v_h)
                z_b = pb_ref[rs, 2 * KEY_B + VAL_B + h * DV_B:2 * KEY_B + VAL_B + (h + 1) * DV_B]
                o_ref[rs, vs] = _bf(_rms(o_h, gla_g_ref[...]) * _silu(z_b))
        return task

    def gate(lo, hi):
        def task():
            gate_ref[:, lo:hi] = _sigmoid(_dot(hb_ref[...], w_in_ref[:, COL_G + lo:COL_G + hi]))
        return task

    def branch_a(lo, hi):
        def task():
            ba_ref[:, lo:hi] = gate_ref[:, lo:hi] * _dot(a_ref[...], w_sq_ref[:, COL_BA + lo:COL_BA + hi])
        return task

    half = TILE_S // 2

    def branch_b(r):
        def task():
            rs = slice(r * half, (r + 1) * half)
            p_b = _dot(o_ref[rs, :], w_sq_ref[:, COL_BB:COL_BB + D_MODEL])
            mg_ref[rs, :] = _bf(ba_ref[rs, :] + gate_ref[rs, D_MODEL:] * p_b)
        return task

    def out_proj(r):
        def task():
            rs = slice(r * half, (r + 1) * half)
            y = xk_ref[rs, :] + _dot(mg_ref[rs, :], w_sq_ref[:, COL_OUT:COL_OUT + D_MODEL])
            out_ref[rs, :] = _rms(y, fin_g_ref[...])
        return task

    pieces = lambda lo, hi: [(c, min(c + PIECE_N, hi)) for c in range(lo, hi, PIECE_N)]
    proj_a = lambda lo, hi: [proj(pa_ref, COL_A, a, b) for a, b in pieces(lo, hi)]
    proj_b = lambda lo, hi: [proj(pb_ref, COL_B, a, b) for a, b in pieces(lo, hi)]
    z_split = 2 * KEY_B + VAL_B + DV_B * (H_B // 2)
    proj_qk = [proj(pb_ref, COL_B, 0, 2 * KEY_B)]
    proj_zl = [proj(pb_ref, COL_B, z_split, N_B + V7X_LANES)]
    gates = [gate(a, b) for a, b in pieces(0, N_G)]
    mix = [mixa_mix(g) for g in range(H_A)]
    rows = range(N_CB)

    order = [out_proj(0), out_proj(1), norm_in, keep_x]
    order += proj_a(D_A, 2 * D_A) + [act_v(r) for r in rows]
    order += proj_a(0, D_A) + [act_u(r) for r in rows]
    order += proj_a(2 * D_A, 3 * D_A) + [act_z(r) for r in rows]
    order += proj_qk + proj_zl + [decay_gate] + proj_b(2 * KEY_B, 2 * KEY_B + VAL_B) + [decay_cumsum]
    order += [gla_scale(c) for c in rows]
    order += proj_b(2 * KEY_B + VAL_B, z_split)
    n_pairs = N_CB // 2
    gates_per, mix_per = len(gates) // n_pairs, len(mix) // N_CB
    for j in range(n_pairs):
        order += [gla_scores(2 * j), gla_scores(2 * j + 1)] + gates[j * gates_per:(j + 1) * gates_per]
        for c in (2 * j, 2 * j + 1):
            order += [gla_out(c)] + mix[c * mix_per:(c + 1) * mix_per]
    order += [branch_a(a, b) for a, b in pieces(0, D_MODEL)]
    order += [branch_b(0), branch_b(1)]
    for task in order:
        task()


def _const_spec(shape):
    nd = len(shape)
    return pl.BlockSpec(shape, lambda t: (0,) * nd, pipeline_mode=pl.Buffered(1))


N_STAGE_SLOTS = 2
N_WEIGHT_STREAMS = 4

_WEIGHT_SCRATCH = (
    ((D_MODEL, W_IN_COLS), jnp.bfloat16),
    ((D_MODEL, W_SQ_COLS), jnp.bfloat16),
    ((N_STAGE_SLOTS, STAGE_COLS_IN, D_MODEL), jnp.float32),
    ((N_STAGE_SLOTS, 3, STAGE_ROWS, D_MODEL), jnp.float32),
)

_SCRATCH = (
    ((H_B, DK_B, DV_B), jnp.float32),
    ((TILE_S, D_MODEL), jnp.bfloat16),
    ((TILE_S, D_MODEL), jnp.float32),
    ((TILE_S, N_A), jnp.float32),
    ((TILE_S, N_B + V7X_LANES), jnp.float32),
    ((TILE_S, N_G), jnp.float32),
    ((TILE_S, D_A), jnp.bfloat16),
    ((TILE_S, D_A), jnp.float32),
    ((TILE_S, KEY_B), jnp.bfloat16),
    ((TILE_S, KEY_B), jnp.bfloat16),
    ((TILE_S, KEY_B), jnp.float32),
    ((TILE_S, KEY_B), jnp.bfloat16),
    ((TILE_S, KEY_B), jnp.bfloat16),
    ((TILE_S, KEY_B), jnp.bfloat16),
    ((TILE_S, KEY_B), jnp.float32),
    ((TILE_S, D_MODEL), jnp.bfloat16),
    ((TILE_S, D_MODEL), jnp.bfloat16),
    ((TILE_S, D_MODEL), jnp.float32),
    ((TILE_S, D_MODEL), jnp.bfloat16),
)


def _vmem_limit_bytes():
    nbytes = lambda shape, dt: math.prod(shape) * jnp.dtype(dt).itemsize
    params = 2 * V7X_LANES * KEY_B + 4 * (H_A * CHUNK_A * CHUNK_A + CHUNK_A * D_A)
    io_windows = 2 * 2 * TILE_S * D_MODEL * 4
    scratch = sum(nbytes(s, d) for s, d in _WEIGHT_SCRATCH + _SCRATCH)
    temporaries = 12 * TILE_S * D_MODEL * 4
    return min(params + io_windows + scratch + temporaries, V7X_VMEM_BYTES - 8 * 1024 * 1024)


@jax.jit
def _forward(x, norm_g, w_in, ln_v_g, ln_v_b, w_spatial, b_spatial, w_gate_up, b_gate_up,
             gla_norm_g, w_branch_a, w_branch_b, w_out, final_norm_g):
    bsz, seq, d = x.shape
    assert d == D_MODEL and seq % TILE_S == 0
    assert TILE_S % CHUNK_A == 0 and TILE_S % CHUNK_B == 0
    assert w_in.shape == (D_MODEL, SRC_G + N_G)
    w_gu = jnp.pad(w_gate_up, ((0, V7X_LANES - GATE_RANK), (0, 0))).astype(jnp.bfloat16)
    b_s = jnp.repeat(b_spatial.T, GROUP_A, axis=1)
    row = lambda a: a.reshape(1, -1)

    hbm_weights = (w_in.T, w_branch_a, w_branch_b, w_out)
    params = (row(ln_v_g), row(ln_v_b), w_spatial, b_s, w_gu, row(b_gate_up), row(gla_norm_g),
              row(final_norm_g))
    tiles_per_row = seq // TILE_S
    n_tiles = bsz * tiles_per_row

    def tile_spec(offset):
        def index_map(t):
            tile = jnp.clip(t + offset, 0, n_tiles - 1)
            return tile // tiles_per_row, lax.rem(tile, tiles_per_row), 0
        return pl.BlockSpec((None, TILE_S, D_MODEL), index_map)

    operands = (x, row(norm_g)) + hbm_weights + params
    in_specs = [tile_spec(0), _const_spec((1, D_MODEL))]
    in_specs += [pl.BlockSpec(memory_space=pl.ANY) for _ in hbm_weights]
    in_specs += [_const_spec(op.shape) for op in params]
    return pl.pallas_call(
        functools.partial(_layer_kernel, tiles_per_row),
        grid=(n_tiles + 1,),
        in_specs=in_specs,
        out_specs=tile_spec(-1),
        out_shape=jax.ShapeDtypeStruct(x.shape, x.dtype),
        scratch_shapes=[pltpu.VMEM(s, d) for s, d in _WEIGHT_SCRATCH]
        + [pltpu.SemaphoreType.DMA((N_STAGE_SLOTS, N_WEIGHT_STREAMS))]
        + [pltpu.VMEM(s, d) for s, d in _SCRATCH],
        compiler_params=pltpu.CompilerParams(
            dimension_semantics=("arbitrary",),
            vmem_limit_bytes=_vmem_limit_bytes(),
        ),
        name="hybrid_layer",
    )(*operands)


def kernel(x, norm_g, w_in, ln_v_g, ln_v_b, w_spatial, b_spatial, w_gate_up, b_gate_up, gla_norm_g,
           w_branch_a, w_branch_b, w_out, final_norm_g):
    assert norm_g.shape[0] == 1, "single-layer problem"
    return _forward(x, norm_g[0], w_in[0], ln_v_g[0], ln_v_b[0], w_spatial[0], b_spatial[0],
                    w_gate_up[0], b_gate_up[0], gla_norm_g[0], w_branch_a[0], w_branch_b[0],
                    w_out[0], final_norm_g)
```

```python
import functools
import math

import jax
import jax.numpy as jnp
from jax import lax
from jax.experimental import pallas as pl
from jax.experimental.pallas import tpu as pltpu

D_MODEL = 1024
D_A = D_MODEL
H_A = 8
GROUP_A = D_A // H_A
CHUNK_A = 128
H_B = 4
KEY_B = D_MODEL // 2
VAL_B = D_MODEL
DK_B = KEY_B // H_B
DV_B = VAL_B // H_B
GATE_RANK = 16
GATE_NORM = 16.0
CHUNK_B = 64
EPS = 1e-6
LN_EPS = 1e-5

V7X_LANES = 128
V7X_SUBLANES = 8
V7X_VMEM_BYTES = 64 * 1024 * 1024
TILE_S = 256
PIECE_N = 512
STAGE_ROWS = 64
STAGE_COLS_IN = 512

N_A = 3 * D_A
N_B = 2 * KEY_B + 2 * VAL_B
N_G = 2 * D_MODEL
N_CA = TILE_S // CHUNK_A
N_CB = TILE_S // CHUNK_B
SRC_G = N_A + N_B + GATE_RANK

COL_A = 0
COL_B = COL_A + N_A
COL_LR = COL_B + N_B
COL_G = COL_LR + V7X_LANES
W_IN_COLS = COL_G + N_G
COL_BA = 0
COL_BB = COL_BA + D_MODEL
COL_OUT = COL_BB + D_MODEL
W_SQ_COLS = COL_OUT + D_MODEL + V7X_LANES
assert (W_IN_COLS // V7X_LANES) % 2 == 1 and (W_SQ_COLS // V7X_LANES) % 2 == 1
assert COL_LR % STAGE_COLS_IN == 0 and N_G % STAGE_COLS_IN == 0


def _bf(x):
    return x.astype(jnp.bfloat16)


def _dot(a, b):
    return jnp.dot(a, b, preferred_element_type=jnp.float32)


def _dot_nt(a, b):
    return lax.dot_general(a, b, (((1,), (1,)), ((), ())), preferred_element_type=jnp.float32)


def _gelu(x):
    return 0.5 * x * (1.0 + lax.erf(x * (1.0 / math.sqrt(2.0))))


def _sigmoid(x):
    return 1.0 / (1.0 + jnp.exp(-x))


def _silu(x):
    return x * _sigmoid(x)


def _log_sigmoid(x):
    return jnp.minimum(x, 0.0) - jnp.log(1.0 + jnp.exp(-jnp.abs(x)))


def _rms(x, g):
    return x * lax.rsqrt(jnp.mean(x * x, axis=-1, keepdims=True) + EPS) * g


def _stage_chunks(n_chunks, copies, store):
    for cp in copies(0, 0):
        cp.start()

    def body(i, carry):
        slot = lax.rem(i, N_STAGE_SLOTS)

        @pl.when(i + 1 < n_chunks)
        def _():
            for cp in copies(i + 1, lax.rem(i + 1, N_STAGE_SLOTS)):
                cp.start()

        for cp in copies(i, slot):
            cp.wait()
        store(i, slot)
        return carry

    lax.fori_loop(0, n_chunks, body, 0)


def _stage_weights(w_in_t_hbm, w_sq_hbm, w_in_ref, w_sq_ref, st_in_ref, st_sq_ref, sem_ref):
    def in_chunks(src0, dst0):
        def copies(i, slot):
            rows = pl.ds(pl.multiple_of(src0 + i * STAGE_COLS_IN, 16), STAGE_COLS_IN)
            return [pltpu.make_async_copy(w_in_t_hbm.at[rows, :], st_in_ref.at[slot], sem_ref.at[slot, 0])]

        def store(i, slot):
            cols = pl.ds(pl.multiple_of(dst0 + i * STAGE_COLS_IN, STAGE_COLS_IN), STAGE_COLS_IN)
            w_in_ref[:, cols] = _bf(st_in_ref[slot].T)
        return copies, store

    def sq_copies(i, slot):
        rows = pl.ds(pl.multiple_of(i * STAGE_ROWS, STAGE_ROWS), STAGE_ROWS)
        return [pltpu.make_async_copy(w.at[rows, :], st_sq_ref.at[slot, k], sem_ref.at[slot, 1 + k])
                for k, w in enumerate(w_sq_hbm)]

    def sq_store(i, slot):
        rows = pl.ds(pl.multiple_of(i * STAGE_ROWS, STAGE_ROWS), STAGE_ROWS)
        for k in range(len(w_sq_hbm)):
            w_sq_ref[rows, k * D_MODEL:(k + 1) * D_MODEL] = _bf(st_sq_ref[slot, k])

    _stage_chunks(COL_LR // STAGE_COLS_IN, *in_chunks(0, 0))
    _stage_chunks(N_G // STAGE_COLS_IN, *in_chunks(SRC_G, COL_G))
    _stage_chunks(D_MODEL // STAGE_ROWS, sq_copies, sq_store)
    lr_copy = pltpu.make_async_copy(w_in_t_hbm.at[pl.ds(COL_LR, GATE_RANK), :],
                                    st_in_ref.at[0, pl.ds(0, GATE_RANK), :], sem_ref.at[0, 0])
    lr_copy.start()
    lr_copy.wait()
    w_in_ref[:, COL_LR:COL_G] = jnp.zeros((D_MODEL, V7X_LANES), jnp.bfloat16)
    w_in_ref[:, COL_LR:COL_LR + GATE_RANK] = _bf(st_in_ref[0, 0:GATE_RANK, :].T)


def _layer_kernel(tiles_per_row,
                  x_ref, norm_g_ref, w_in_hbm, w_ba_hbm, w_bb_hbm, w_out_hbm,
                  ln_g_ref, ln_b_ref, w_s_ref, b_s_ref, w_gu_ref, b_gu_ref, gla_g_ref, fin_g_ref,
                  out_ref,
                  w_in_ref, w_sq_ref, st_in_ref, st_sq_ref, sem_ref,
                  state_ref, hb_ref, xk_ref, pa_ref, pb_ref, gate_ref, vn_ref, ga_ref, gh_ref, gl_ref, b_ref,
                  qi_ref, ki_ref, qs_ref, ks_ref, a_ref, o_ref, ba_ref, mg_ref):
    step = pl.program_id(0)

    @pl.when(step == 0)
    def _():
        _stage_weights(w_in_hbm, (w_ba_hbm, w_bb_hbm, w_out_hbm), w_in_ref, w_sq_ref,
                       st_in_ref, st_sq_ref, sem_ref)
        mg_ref[...] = jnp.zeros_like(mg_ref)
        xk_ref[...] = jnp.zeros_like(xk_ref)

    @pl.when(lax.rem(step, tiles_per_row) == 0)
    def _():
        state_ref[...] = jnp.zeros_like(state_ref)

    def norm_in():
        x = x_ref[...]
        hb_ref[...] = _bf(_rms(x, norm_g_ref[...]))

    def keep_x():
        xk_ref[...] = x_ref[...]

    def proj(dst_ref, col0, lo, hi):
        def task():
            dst_ref[:, lo:hi] = _dot(hb_ref[...], w_in_ref[:, col0 + lo:col0 + hi])
        return task

    def act_v(r):
        def task():
            rs = slice(r * CHUNK_B, (r + 1) * CHUNK_B)
            v = _gelu(pa_ref[rs, D_A:2 * D_A])
            mu = jnp.mean(v, axis=-1, keepdims=True)
            vc = v - mu
            inv = lax.rsqrt(jnp.mean(vc * vc, axis=-1, keepdims=True) + LN_EPS)
            vn_ref[rs, :] = _bf(vc * inv * ln_g_ref[...] + ln_b_ref[...])
        return task

    def act_uz(r):
        def task():
            rs = slice(r * CHUNK_B, (r + 1) * CHUNK_B)
            ga_ref[rs, :] = _gelu(pa_ref[rs, 0:D_A]) * _silu(pa_ref[rs, 2 * D_A:3 * D_A])
        return task

    def mixa_mix(g):
        def task():
            row = lax.broadcasted_iota(jnp.int32, (CHUNK_A, CHUNK_A), 0)
            col = lax.broadcasted_iota(jnp.int32, (CHUNK_A, CHUNK_A), 1)
            w_g = _bf(jnp.where(row >= col, w_s_ref[g], 0.0))
            cs = slice(g * GROUP_A, (g + 1) * GROUP_A)
            chunks = [slice(c * CHUNK_A, (c + 1) * CHUNK_A) for c in range(N_CA)]
            mixed = _dot(w_g, jnp.concatenate([vn_ref[rs, cs] for rs in chunks], axis=1))
            for c, rs in enumerate(chunks):
                m_c = mixed[:, c * GROUP_A:(c + 1) * GROUP_A] + b_s_ref[:, cs]
                a_ref[rs, cs] = _bf(ga_ref[rs, cs] * m_c)
        return task

    def decay_gate():
        logit = _dot(_bf(pb_ref[:, N_B:N_B + V7X_LANES]), w_gu_ref[...]) + b_gu_ref[...]
        log_alpha = _log_sigmoid(logit) * (1.0 / GATE_NORM)
        g_hi = _bf(log_alpha)
        gh_ref[...] = g_hi
        gl_ref[...] = _bf(log_alpha - g_hi.astype(jnp.float32))

    def decay_cumsum():
        ri = lax.broadcasted_iota(jnp.int32, (TILE_S, TILE_S), 0)
        ci = lax.broadcasted_iota(jnp.int32, (TILE_S, TILE_S), 1)
        tril = _bf(jnp.where((ri >= ci) & (ri // CHUNK_B == ci // CHUNK_B), 1.0, 0.0))
        b_ref[...] = _dot(tril, gh_ref[...]) + _dot(tril, gl_ref[...])

    def gla_scale(c):
        def task():
            rs = slice(c * CHUNK_B, (c + 1) * CHUNK_B)
            b = b_ref[rs, :]
            b_mid = b[CHUNK_B // 2 - 1:CHUNK_B // 2]
            b_last = b[CHUNK_B - 1:CHUNK_B]
            q = pb_ref[rs, 0:KEY_B] * (DK_B ** -0.5)
            k = pb_ref[rs, KEY_B:2 * KEY_B]
            qi_ref[rs, :] = _bf(q * jnp.exp(b - b_mid))
            ki_ref[rs, :] = _bf(k * jnp.exp(b_mid - b))
            qs_ref[rs, :] = _bf(q * jnp.exp(b))
            ks_ref[rs, :] = k * jnp.exp(b_last - b)
        return task

    scores = {}

    def gla_scores(c):
        def task():
            rs = slice(c * CHUNK_B, (c + 1) * CHUNK_B)
            rc = lax.broadcasted_iota(jnp.int32, (CHUNK_B, CHUNK_B), 0)
            cc = lax.broadcasted_iota(jnp.int32, (CHUNK_B, CHUNK_B), 1)
            for h in range(H_B):
                ks = slice(h * DK_B, (h + 1) * DK_B)
                scores[c, h] = _bf(jnp.where(rc >= cc, _dot_nt(qi_ref[rs, ks], ki_ref[rs, ks]), 0.0))
        return task

    def gla_out(c):
        def task():
            rs = slice(c * CHUNK_B, (c + 1) * CHUNK_B)
            for h in range(H_B):
                ks = slice(h * DK_B, (h + 1) * DK_B)
                vs = slice(h * DV_B, (h + 1) * DV_B)
                st = state_ref[h]
                v_h = _bf(pb_ref[rs, 2 * KEY_B + h * DV_B:2 * KEY_B + (h + 1) * DV_B])
                o_h = _dot(scores[c, h], v_h) + _dot(qs_ref[rs, ks], _bf(st))
                decay_row = jnp.exp(b_ref[(c + 1) * CHUNK_B - 1:(c + 1) * CHUNK_B, ks])
                decay_col = jnp.broadcast_to(decay_row, (V7X_SUBLANES, DK_B)).T[:, 0:1]
                state_ref[h] = st * decay_col + _dot(_bf(ks_ref[rs, ks].T), v_h)
                z_b = pb_ref[rs, 2 * KEY_B + VAL_B + h * DV_B:2 * KEY_B + VAL_B + (h + 1) * DV_B]
                o_ref[rs, vs] = _bf(_rms(o_h, gla_g_ref[...]) * _silu(z_b))
        return task

    def gate(lo, hi):
        def task():
            gate_ref[:, lo:hi] = _sigmoid(_dot(hb_ref[...], w_in_ref[:, COL_G + lo:COL_G + hi]))
        return task

    def branch_a(lo, hi):
        def task():
            ba_ref[:, lo:hi] = gate_ref[:, lo:hi] * _dot(a_ref[...], w_sq_ref[:, COL_BA + lo:COL_BA + hi])
        return task

    half = TILE_S // 2

    def branch_b(r):
        def task():
            rs = slice(r * half, (r + 1) * half)
            p_b = _dot(o_ref[rs, :], w_sq_ref[:, COL_BB:COL_BB + D_MODEL])
            mg_ref[rs, :] = _bf(ba_ref[rs, :] + gate_ref[rs, D_MODEL:] * p_b)
        return task

    def out_proj(r):
        def task():
            rs = slice(r * half, (r + 1) * half)
            y = xk_ref[rs, :] + _dot(mg_ref[rs, :], w_sq_ref[:, COL_OUT:COL_OUT + D_MODEL])
            out_ref[rs, :] = _rms(y, fin_g_ref[...])
        return task

    pieces = lambda lo, hi: [(c, min(c + PIECE_N, hi)) for c in range(lo, hi, PIECE_N)]
    proj_a = lambda lo, hi: [proj(pa_ref, COL_A, a, b) for a, b in pieces(lo, hi)]
    proj_b = lambda lo, hi: [proj(pb_ref, COL_B, a, b) for a, b in pieces(lo, hi)]
    z_split = 2 * KEY_B + VAL_B + DV_B * (H_B // 2)
    proj_qk = [proj(pb_ref, COL_B, 0, 2 * KEY_B)]
    proj_zl = [proj(pb_ref, COL_B, z_split, N_B + V7X_LANES)]
    gates = [gate(a, b) for a, b in pieces(0, N_G)]
    mix = [mixa_mix(g) for g in range(H_A)]
    rows = range(N_CB)

    order = [out_proj(0), out_proj(1), norm_in, keep_x]
    order += proj_a(D_A, 2 * D_A) + [act_v(r) for r in rows]
    order += proj_a(0, D_A) + proj_a(2 * D_A, 3 * D_A) + [act_uz(r) for r in rows]
    order += proj_qk + proj_zl + [decay_gate] + proj_b(2 * KEY_B, 2 * KEY_B + VAL_B) + [decay_cumsum]
    order += [gla_scale(c) for c in rows]
    order += proj_b(2 * KEY_B + VAL_B, z_split)
    n_pairs = N_CB // 2
    gates_per, mix_per = len(gates) // n_pairs, len(mix) // N_CB
    for j in range(n_pairs):
        order += [gla_scores(2 * j), gla_scores(2 * j + 1)] + gates[j * gates_per:(j + 1) * gates_per]
        for c in (2 * j, 2 * j + 1):
            order += [gla_out(c)] + mix[c * mix_per:(c + 1) * mix_per]
    order += [branch_a(a, b) for a, b in pieces(0, D_MODEL)]
    order += [branch_b(0), branch_b(1)]
    for task in order:
        task()


def _const_spec(shape):
    nd = len(shape)
    return pl.BlockSpec(shape, lambda t: (0,) * nd, pipeline_mode=pl.Buffered(1))


N_STAGE_SLOTS = 2
N_WEIGHT_STREAMS = 4

_WEIGHT_SCRATCH = (
    ((D_MODEL, W_IN_COLS), jnp.bfloat16),
    ((D_MODEL, W_SQ_COLS), jnp.bfloat16),
    ((N_STAGE_SLOTS, STAGE_COLS_IN, D_MODEL), jnp.float32),
    ((N_STAGE_SLOTS, 3, STAGE_ROWS, D_MODEL), jnp.float32),
)

_SCRATCH = (
    ((H_B, DK_B, DV_B), jnp.float32),
    ((TILE_S, D_MODEL), jnp.bfloat16),
    ((TILE_S, D_MODEL), jnp.float32),
    ((TILE_S, N_A), jnp.float32),
    ((TILE_S, N_B + V7X_LANES), jnp.float32),
    ((TILE_S, N_G), jnp.float32),
    ((TILE_S, D_A), jnp.bfloat16),
    ((TILE_S, D_A), jnp.float32),
    ((TILE_S, KEY_B), jnp.bfloat16),
    ((TILE_S, KEY_B), jnp.bfloat16),
    ((TILE_S, KEY_B), jnp.float32),
    ((TILE_S, KEY_B), jnp.bfloat16),
    ((TILE_S, KEY_B), jnp.bfloat16),
    ((TILE_S, KEY_B), jnp.bfloat16),
    ((TILE_S, KEY_B), jnp.float32),
    ((TILE_S, D_MODEL), jnp.bfloat16),
    ((TILE_S, D_MODEL), jnp.bfloat16),
    ((TILE_S, D_MODEL), jnp.float32),
    ((TILE_S, D_MODEL), jnp.bfloat16),
)


def _vmem_limit_bytes():
    nbytes = lambda shape, dt: math.prod(shape) * jnp.dtype(dt).itemsize
    params = 2 * V7X_LANES * KEY_B + 4 * (H_A * CHUNK_A * CHUNK_A + CHUNK_A * D_A)
    io_windows = 2 * 2 * TILE_S * D_MODEL * 4
    scratch = sum(nbytes(s, d) for s, d in _WEIGHT_SCRATCH + _SCRATCH)
    temporaries = 12 * TILE_S * D_MODEL * 4
    return min(params + io_windows + scratch + temporaries, V7X_VMEM_BYTES - 8 * 1024 * 1024)


@jax.jit
def _forward(x, norm_g, w_in, ln_v_g, ln_v_b, w_spatial, b_spatial, w_gate_up, b_gate_up,
             gla_norm_g, w_branch_a, w_branch_b, w_out, final_norm_g):
    bsz, seq, d = x.shape
    assert d == D_MODEL and seq % TILE_S == 0
    assert TILE_S % CHUNK_A == 0 and TILE_S % CHUNK_B == 0
    assert w_in.shape == (D_MODEL, SRC_G + N_G)
    w_gu = jnp.pad(w_gate_up, ((0, V7X_LANES - GATE_RANK), (0, 0))).astype(jnp.bfloat16)
    b_s = jnp.repeat(b_spatial.T, GROUP_A, axis=1)
    row = lambda a: a.reshape(1, -1)

    hbm_weights = (w_in.T, w_branch_a, w_branch_b, w_out)
    params = (row(ln_v_g), row(ln_v_b), w_spatial, b_s, w_gu, row(b_gate_up), row(gla_norm_g),
              row(final_norm_g))
    tiles_per_row = seq // TILE_S
    n_tiles = bsz * tiles_per_row

    def tile_spec(offset):
        def index_map(t):
            tile = jnp.clip(t + offset, 0, n_tiles - 1)
            return tile // tiles_per_row, lax.rem(tile, tiles_per_row), 0
        return pl.BlockSpec((None, TILE_S, D_MODEL), index_map)

    operands = (x, row(norm_g)) + hbm_weights + params
    in_specs = [tile_spec(0), _const_spec((1, D_MODEL))]
    in_specs += [pl.BlockSpec(memory_space=pl.ANY) for _ in hbm_weights]
    in_specs += [_const_spec(op.shape) for op in params]
    return pl.pallas_call(
        functools.partial(_layer_kernel, tiles_per_row),
        grid=(n_tiles + 1,),
        in_specs=in_specs,
        out_specs=tile_spec(-1),
        out_shape=jax.ShapeDtypeStruct(x.shape, x.dtype),
        scratch_shapes=[pltpu.VMEM(s, d) for s, d in _WEIGHT_SCRATCH]
        + [pltpu.SemaphoreType.DMA((N_STAGE_SLOTS, N_WEIGHT_STREAMS))]
        + [pltpu.VMEM(s, d) for s, d in _SCRATCH],
        compiler_params=pltpu.CompilerParams(
            dimension_semantics=("arbitrary",),
            vmem_limit_bytes=_vmem_limit_bytes(),
        ),
        name="hybrid_layer",
    )(*operands)


def kernel(x, norm_g, w_in, ln_v_g, ln_v_b, w_spatial, b_spatial, w_gate_up, b_gate_up, gla_norm_g,
           w_branch_a, w_branch_b, w_out, final_norm_g):
    assert norm_g.shape[0] == 1, "single-layer problem"
    return _forward(x, norm_g[0], w_in[0], ln_v_g[0], ln_v_b[0], w_spatial[0], b_spatial[0],
                    w_gate_up[0], b_gate_up[0], gla_norm_g[0], w_branch_a[0], w_branch_b[0],
                    w_out[0], final_norm_g)
```
